```python
import functools
import jax, jax.numpy as jnp
from jax import lax
import numpy as np

D_MODEL = 2048
BATCH = 8
SEQ = 2048
DEPTH = 1
DEC_BATCH = 32
DEC_SEQ = 4
PAST_LEN = 8192
PAGE_SIZE = 128

HEAD_DIM_A = 128
N_HEADS_A = D_MODEL // 256
C_A = N_HEADS_A * HEAD_DIM_A
IDX_HEADS = 8
IDX_DIM = 64
TOPK_MAX = 256
Q_BLOCK = 32
HEAD_B = 64
N_HEADS_B = D_MODEL // 128
C_B = N_HEADS_B * HEAD_B
D_DECAY = 64
D_AAA = 64
D_GATE = 160
LNX_EPS = 64e-5
D_FF = -(-8 * D_MODEL // (3 * 256)) * 256
ROPE_THETA = 500000.0
ROPE_FRACTION = 4
NORM_EPS = 1e-6
COLS_A = [C_A, C_A, C_A, IDX_HEADS * IDX_DIM, IDX_DIM, IDX_HEADS, D_MODEL, D_MODEL]
COLS_B = [C_B, C_B, C_B, D_DECAY, D_AAA, D_GATE]
N_IN_A = 3 * C_A + IDX_HEADS * IDX_DIM + IDX_DIM + IDX_HEADS + 2 * D_MODEL
N_IN_B = 3 * C_B + D_DECAY + D_AAA + D_GATE

kernel_name = 'hybrid_dsa_rwkv7_decoder_step'


def _rmsnorm(x, g):
    xf = x.astype(jnp.float32)
    y = xf * lax.rsqrt(jnp.mean(xf * xf, axis=-1, keepdims=True) + NORM_EPS)
    return (y * g.astype(jnp.float32)).astype(x.dtype)


def _split_last(t, sizes):
    out, o = [], 0
    for s in sizes:
        out.append(t[..., o:o + s])
        o += s
    return out


def _rope(x, pos):
    d = x.shape[-1]
    rot = d // ROPE_FRACTION
    half = rot // 2
    inv = ROPE_THETA ** (-jnp.arange(half, dtype=jnp.float32) * 2.0 / rot)
    ang = pos.astype(jnp.float32)[:, None] * inv[None, :]
    cos = jnp.cos(ang)[:, None, :].astype(x.dtype)
    sin = jnp.sin(ang)[:, None, :].astype(x.dtype)
    x1, x2 = x[..., :half], x[..., half:rot]
    return jnp.concatenate([x1 * cos - x2 * sin, x2 * cos + x1 * sin, x[..., rot:]], axis=-1)


def _gather_rows(src, idx):
    return jax.vmap(lambda s, i: s[i])(src, idx)


def _indexer_scores(qi, wi, ki):
    logits = jnp.einsum('bthd,bsd->bths', qi, ki).astype(jnp.float32)
    return jnp.einsum('bths,bth->bts', jax.nn.relu(logits), wi.astype(jnp.float32))


def _select(scores, q_pos, n_sel):
    L = scores.shape[-1]
    future = jnp.arange(L)[None, None, :] > q_pos[None, :, None]
    s = jnp.where(future, -1e30, scores)
    _, idx = lax.top_k(s, n_sel)
    return idx, idx <= q_pos[None, :, None]


def _attend(q, k_sel, v_sel, valid):
    s = jnp.einsum('bthd,btkhd->bthk', q, k_sel).astype(jnp.float32) * (HEAD_DIM_A ** -0.5)
    s = jnp.where(valid[:, :, None, :], s, -1e30)
    p = jax.nn.softmax(s, axis=-1).astype(v_sel.dtype)
    return jnp.einsum('bthk,btkhd->bthd', p, v_sel)


def _attn_prompt(q, k, v, qi, ki, wi):
    B, S = q.shape[:2]
    n_sel = min(TOPK_MAX, S // 4)
    qb = min(Q_BLOCK, S)
    nb = S // qb

    def to_blocks(t):
        return jnp.moveaxis(t.reshape((B, nb, qb) + t.shape[2:]), 1, 0)

    def block(args):
        q_b, qi_b, wi_b, start = args
        q_pos = start + jnp.arange(qb)
        idx, valid = _select(_indexer_scores(qi_b, wi_b, ki), q_pos, n_sel)
        return _attend(q_b, _gather_rows(k, idx), _gather_rows(v, idx), valid)

    out = lax.map(block, (to_blocks(q), to_blocks(qi), to_blocks(wi), jnp.arange(nb) * qb))
    return jnp.moveaxis(out, 0, 1).reshape(q.shape)


def _attn_sample(q, k, v, qi, ki, wi, cache_k, cache_v, cache_ki, page_table, layer):
    B, T = q.shape[:2]
    past = page_table.shape[1] * PAGE_SIZE
    n_sel = min(TOPK_MAX, (past + T) // 4)
    ki_past = cache_ki[layer, page_table].reshape(B, past, IDX_DIM).astype(ki.dtype)
    ki_all = jnp.concatenate([ki_past, ki], axis=1)
    q_pos = past + jnp.arange(T)
    idx, valid = _select(_indexer_scores(qi, wi, ki_all), q_pos, n_sel)
    in_past = (idx < past)[..., None, None]
    pidx = jnp.minimum(idx, past - 1)
    phys_page = jax.vmap(lambda pt, pg: pt[pg])(page_table, pidx // PAGE_SIZE)
    slot = pidx % PAGE_SIZE
    nidx = jnp.clip(idx - past, 0, T - 1)
    k_sel = jnp.where(in_past, cache_k[layer, phys_page, slot].astype(k.dtype), _gather_rows(k, nidx))
    v_sel = jnp.where(in_past, cache_v[layer, phys_page, slot].astype(v.dtype), _gather_rows(v, nidx))
    return _attend(q, k_sel, v_sel, valid)


def _rwkv7(pb, pb_prev, wkv0, p):
    B, T, _ = pb.shape
    f32 = jnp.float32
    xm = pb + (pb_prev - pb) * p['mu_b']
    r, k, v, xw, xa, xg = _split_last(xm, COLS_B)
    w = -jax.nn.softplus(-(p['w0'] + jnp.tanh(xw) @ p['w2'])) - 0.5
    decay = jnp.exp(-jnp.exp(w.astype(f32)))
    a = jax.nn.sigmoid(p['a0'] + xa @ p['a2'])
    g = jax.nn.sigmoid(xg) @ p['g2']
    heads = lambda t: t.reshape(B, T, N_HEADS_B, HEAD_B).astype(f32)
    kk = heads(k * p['k_k'])
    kk = kk / jnp.maximum(jnp.linalg.norm(kk, axis=-1, keepdims=True), 1e-12)
    k = k * (1 + (a - 1) * p['k_a'])
    r_h, k_h, v_h, a_h, dec_h = heads(r), heads(k), heads(v), heads(a), heads(decay)

    def step(S, inp):
        r_t, dec_t, k_t, v_t, a_t, b_t = inp
        sa = jnp.einsum('bhij,bhj->bhi', S, a_t)
        S = S * dec_t[:, :, None, :] + sa[..., None] * b_t[:, :, None, :] + v_t[..., None] * k_t[:, :, None, :]
        return S, jnp.einsum('bhij,bhj->bhi', S, r_t)

    tm = lambda t: jnp.moveaxis(t, 1, 0)
    S_T, y = lax.scan(step, wkv0.astype(f32),
                      (tm(r_h), tm(dec_h), tm(k_h), tm(v_h), tm(-kk), tm(kk * a_h)))
    y = jnp.moveaxis(y, 0, 1)
    mean = jnp.mean(y, axis=-1, keepdims=True)
    var = jnp.mean(jnp.square(y - mean), axis=-1, keepdims=True)
    yn = ((y - mean) * lax.rsqrt(var + LNX_EPS)).reshape(B, T, C_B) * p['lnx_w'] + p['lnx_b']
    bonus = (jnp.sum(r_h * k_h * p['r_k'], axis=-1, keepdims=True) * v_h).reshape(B, T, C_B)
    out = ((yn + bonus) * g).astype(pb.dtype)
    return out, S_T.astype(wkv0.dtype)


def _layer(x, h_prev, wkv0, pos0, attn_fn, p):
    B, T, _ = x.shape
    h = _rmsnorm(x, p['g_mix'])
    proj = h @ p['w_in']
    qa, ka, va, qi, ki, wi, ga, gb = _split_last(proj[..., :N_IN_A], COLS_A)
    pos = pos0 + jnp.arange(T)
    qa = _rope(qa.reshape(B, T, N_HEADS_A, HEAD_DIM_A), pos)
    ka = _rope(ka.reshape(B, T, N_HEADS_A, HEAD_DIM_A), pos)
    va = va.reshape(B, T, N_HEADS_A, HEAD_DIM_A)
    qi = _rope(qi.reshape(B, T, IDX_HEADS, IDX_DIM), pos)
    ki = _rope(ki.reshape(B, T, 1, IDX_DIM), pos)[:, :, 0]
    wi = wi * (IDX_HEADS ** -0.5 * IDX_DIM ** -0.5)
    o_a = attn_fn(qa, ka, va, qi, ki, wi).reshape(B, T, C_A)
    pb = proj[..., N_IN_A:]
    pb_first = h_prev @ p['w_in'][:, N_IN_A:]
    pb_prev = jnp.concatenate([pb_first[:, None, :], pb[:, :-1]], axis=1)
    o_b, wkv = _rwkv7(pb, pb_prev, wkv0, p)
    mixed = jax.nn.sigmoid(ga) * (o_a @ p['w_pa']) + jax.nn.sigmoid(gb) * (o_b @ p['w_pb'])
    x = x + mixed @ p['w_o']
    h2 = _rmsnorm(x, p['g_ffn'])
    x = x + (jax.nn.silu(h2 @ p['w_gate']) * (h2 @ p['w_up'])) @ p['w_down']
    return x, ka, va, ki, wkv, h[:, -1]


def setup_inputs(seed: int = 0) -> dict:
    key = jax.random.key(seed)
    ks = jax.random.split(key, 32)
    f32 = jnp.float32
    nrm = lambda k, shape, scale: scale * jax.random.normal(k, shape, f32)
    n_pages = PAST_LEN // PAGE_SIZE
    n_used = DEC_BATCH * n_pages
    n_pool = n_used + max(1, n_used // 4)
    page_table = jax.random.permutation(ks[7], n_pool)[:n_used].reshape(DEC_BATCH, n_pages).astype(jnp.int32)
    return {
        'x_prompt': nrm(ks[0], (BATCH, SEQ, D_MODEL), 1.0),
        'x_sample': nrm(ks[1], (DEC_BATCH, DEC_SEQ, D_MODEL), 1.0),
        'cache_k': nrm(ks[2], (DEPTH, n_pool, PAGE_SIZE, N_HEADS_A, HEAD_DIM_A), 1.0),
        'cache_v': nrm(ks[3], (DEPTH, n_pool, PAGE_SIZE, N_HEADS_A, HEAD_DIM_A), 1.0),
        'cache_kidx': nrm(ks[4], (DEPTH, n_pool, PAGE_SIZE, IDX_DIM), 1.0),
        'state_wkv': nrm(ks[5], (DEPTH, DEC_BATCH, N_HEADS_B, HEAD_B, HEAD_B), 0.5),
        'state_shift': nrm(ks[6], (DEPTH, DEC_BATCH, D_MODEL), 1.0),
        'page_table': page_table,
        'g_mix': 1.0 + nrm(ks[8], (DEPTH, D_MODEL), 0.02),
        'w_in': nrm(ks[9], (DEPTH, D_MODEL, N_IN_A + N_IN_B), D_MODEL ** -0.5),
        'mu_b': jax.random.uniform(ks[10], (DEPTH, N_IN_B), f32),
        'w0': jax.random.uniform(ks[11], (DEPTH, C_B), f32, -6.0, -1.0),
        'w2': nrm(ks[12], (DEPTH, D_DECAY, C_B), D_DECAY ** -0.5),
        'a0': nrm(ks[13], (DEPTH, C_B), 0.1),
        'a2': nrm(ks[14], (DEPTH, D_AAA, C_B), D_AAA ** -0.5),
        'g2': nrm(ks[15], (DEPTH, D_GATE, C_B), D_GATE ** -0.5),
        'k_k': 0.85 + nrm(ks[16], (DEPTH, C_B), 0.02),
        'k_a': 1.0 + nrm(ks[17], (DEPTH, C_B), 0.02),
        'r_k': nrm(ks[18], (DEPTH, N_HEADS_B, HEAD_B), 0.1),
        'lnx_w': 1.0 + nrm(ks[19], (DEPTH, C_B), 0.02),
        'lnx_b': nrm(ks[20], (DEPTH, C_B), 0.01),
        'w_pa': nrm(ks[21], (DEPTH, C_A, D_MODEL), C_A ** -0.5),
        'w_pb': nrm(ks[22], (DEPTH, C_B, D_MODEL), C_B ** -0.5),
        'w_o': nrm(ks[23], (DEPTH, D_MODEL, D_MODEL), D_MODEL ** -0.5),
        'g_ffn': 1.0 + nrm(ks[24], (DEPTH, D_MODEL), 0.02),
        'w_gate': nrm(ks[25], (DEPTH, D_MODEL, D_FF), D_MODEL ** -0.5),
        'w_up': nrm(ks[26], (DEPTH, D_MODEL, D_FF), D_MODEL ** -0.5),
        'w_down': nrm(ks[27], (DEPTH, D_FF, D_MODEL), D_FF ** -0.5),
        'g_final': 1.0 + nrm(ks[28], (D_MODEL,), 0.02),
    }


def reference(x_prompt, x_sample, cache_k, cache_v, cache_kidx, state_wkv, state_shift, page_table,
              g_mix, w_in, mu_b, w0, w2, a0, a2, g2, k_k, k_a, r_k, lnx_w, lnx_b,
              w_pa, w_pb, w_o, g_ffn, w_gate, w_up, w_down, g_final):
    B = x_prompt.shape[0]
    past_len = page_table.shape[1] * PAGE_SIZE
    xp, xs = x_prompt, x_sample
    kp_l, vp_l, kip_l, wkvp_l, shp_l = [], [], [], [], []
    ks_l, vs_l, kis_l, wkvs_l, shs_l = [], [], [], [], []
    for l in range(DEPTH):
        p = {'g_mix': g_mix[l], 'w_in': w_in[l], 'mu_b': mu_b[l], 'w0': w0[l], 'w2': w2[l],
             'a0': a0[l], 'a2': a2[l], 'g2': g2[l], 'k_k': k_k[l], 'k_a': k_a[l], 'r_k': r_k[l],
             'lnx_w': lnx_w[l], 'lnx_b': lnx_b[l], 'w_pa': w_pa[l], 'w_pb': w_pb[l], 'w_o': w_o[l],
             'g_ffn': g_ffn[l], 'w_gate': w_gate[l], 'w_up': w_up[l], 'w_down': w_down[l]}
        h0 = jnp.zeros((B, D_MODEL), xp.dtype)
        s0 = jnp.zeros((B, N_HEADS_B, HEAD_B, HEAD_B), xp.dtype)
        xp, kp, vp, kip, wkvp, shp = _layer(xp, h0, s0, 0, _attn_prompt, p)
        attn_s = functools.partial(_attn_sample, cache_k=cache_k, cache_v=cache_v, cache_ki=cache_kidx,
                                   page_table=page_table, layer=l)
        xs, ksm, vsm, kism, wkvs, shs = _layer(xs, state_shift[l], state_wkv[l], past_len, attn_s, p)
        kp_l.append(kp); vp_l.append(vp); kip_l.append(kip); wkvp_l.append(wkvp); shp_l.append(shp)
        ks_l.append(ksm); vs_l.append(vsm); kis_l.append(kism); wkvs_l.append(wkvs); shs_l.append(shs)
    y_prompt = _rmsnorm(xp, g_final)
    y_sample = _rmsnorm(xs, g_final)
    k_prompt = jnp.stack(kp_l)
    v_prompt = jnp.stack(vp_l)
    kidx_prompt = jnp.stack(kip_l)
    wkv_prompt = jnp.stack(wkvp_l)
    shift_prompt = jnp.stack(shp_l)
    k_sample = jnp.stack(ks_l)
    v_sample = jnp.stack(vs_l)
    kidx_sample = jnp.stack(kis_l)
    wkv_sample = jnp.stack(wkvs_l)
    shift_sample = jnp.stack(shs_l)
    return (y_prompt, y_sample, k_prompt, v_prompt, kidx_prompt, wkv_prompt, shift_prompt,
            k_sample, v_sample, kidx_sample, wkv_sample, shift_sample)
```

```python
import functools

import numpy as np
import jax
import jax.numpy as jnp
from jax import lax
from jax.experimental import pallas as pl
from jax.experimental.pallas import tpu as pltpu

F32 = jnp.float32
BF16 = jnp.bfloat16
I32 = jnp.int32

D_MODEL = 2048
HEAD_A = 128
N_HEADS_A = 8
C_A = N_HEADS_A * HEAD_A
IDX_HEADS = 8
IDX_DIM = 64
TOPK = 256
HEAD_B = 64
N_HEADS_B = 16
C_B = N_HEADS_B * HEAD_B
D_DECAY = 64
D_AAA = 64
D_GATE = 160
LNX_EPS = 64e-5
ROPE_THETA = 500000.0
NORM_EPS = 1e-6
PAGE = 128
NEG = -1e30
INT_MIN = -(2 ** 31)

LANES = 128
VMEM_LIMIT = 56 * 1024 * 1024
CHUNK = 64
N_PAIRS = N_HEADS_B // 2
TQ = 256
RANK_BLK = 256
PB_W = 3 * C_B + 128 + 128 + 256
IDX_W = 640


def _cparams(sem):
    return pltpu.CompilerParams(dimension_semantics=sem, vmem_limit_bytes=VMEM_LIMIT)


def _split_bf16(x):
    hi = x.astype(BF16)
    lo = (x - hi.astype(F32)).astype(BF16)
    return hi, lo


def _dot(a, b):
    return jnp.dot(a, b, preferred_element_type=F32)


def _dot_nt(a, b):
    return lax.dot_general(a, b, (((1,), (1,)), ((), ())), preferred_element_type=F32)


def _dot3w(x, wh_ref, wl_ref):
    xh, xl = _split_bf16(x)
    wh = wh_ref[...]
    return _dot(xh, wh) + _dot(xl, wh) + _dot(xh, wl_ref[...])


def _norm_split_kernel(x_ref, g_ref, hi_ref, lo_ref):
    x = x_ref[...]
    y = x * lax.rsqrt(jnp.mean(x * x, axis=-1, keepdims=True) + NORM_EPS) * g_ref[...]
    hi = y.astype(BF16)
    hi_ref[...] = hi
    lo_ref[...] = (y - hi.astype(F32)).astype(BF16)


def _norm_f32_kernel(x_ref, g_ref, o_ref):
    x = x_ref[...]
    o_ref[...] = x * lax.rsqrt(jnp.mean(x * x, axis=-1, keepdims=True) + NORM_EPS) * g_ref[...]


def _row_tile(m):
    return 512 if m % 512 == 0 else m


def _norm_split(x, g):
    m, d = x.shape
    tm = _row_tile(m)
    return pl.pallas_call(
        _norm_split_kernel,
        out_shape=(jax.ShapeDtypeStruct((m, d), BF16), jax.ShapeDtypeStruct((m, d), BF16)),
        grid=(m // tm,),
        in_specs=[pl.BlockSpec((tm, d), lambda i: (i, 0)), pl.BlockSpec((1, d), lambda i: (0, 0))],
        out_specs=(pl.BlockSpec((tm, d), lambda i: (i, 0)), pl.BlockSpec((tm, d), lambda i: (i, 0))),
        compiler_params=_cparams(("parallel",)),
        name="norm_split",
    )(x, g.reshape(1, d))


def _norm_f32(x, g):
    m, d = x.shape
    tm = _row_tile(m)
    return pl.pallas_call(
        _norm_f32_kernel,
        out_shape=jax.ShapeDtypeStruct((m, d), F32),
        grid=(m // tm,),
        in_specs=[pl.BlockSpec((tm, d), lambda i: (i, 0)), pl.BlockSpec((1, d), lambda i: (0, 0))],
        out_specs=pl.BlockSpec((tm, d), lambda i: (i, 0)),
        compiler_params=_cparams(("parallel",)),
        name="norm_f32",
    )(x, g.reshape(1, d))


def _rope_tables(pos, head_dim):
    rot = head_dim // 4
    half = rot // 2
    inv = ROPE_THETA ** (-jnp.arange(half, dtype=F32) * 2.0 / rot)
    ang = pos.astype(F32)[:, None] * inv[None, :]
    cos, sin = jnp.cos(ang), jnp.sin(ang)
    t = pos.shape[0]
    c = jnp.ones((t, head_dim), F32).at[:, :half].set(cos).at[:, half:rot].set(cos)
    s1 = jnp.zeros((t, head_dim), F32).at[:, half:rot].set(sin)
    s2 = jnp.zeros((t, head_dim), F32).at[:, :half].set(-sin)
    rep = LANES // head_dim
    return tuple(jnp.tile(a, (1, rep)) for a in (c, s1, s2))


def _rope_lanes(x, c, s1, s2, half):
    return x * c + pltpu.roll(x, half, axis=1) * s1 + pltpu.roll(x, LANES - half, axis=1) * s2


def _proj_kernel(a_ref, w_ref, c_ref, s1_ref, s2_ref, *out_refs, rope, want_f32, want_bf16):
    acc = _dot(a_ref[...], w_ref[...])
    if rope:
        c, s1, s2 = c_ref[...], s1_ref[...], s2_ref[...]
        parts = [_rope_lanes(acc[:, h * LANES:(h + 1) * LANES], c, s1, s2, HEAD_A // 8)
                 for h in range(acc.shape[1] // LANES)]
        acc = jnp.concatenate(parts, axis=1)
    k = 0
    if want_f32:
        out_refs[k][...] = acc
        k += 1
    if want_bf16:
        out_refs[k][...] = acc.astype(BF16)


def _proj(a, w, tabs, *, rope, want_f32, want_bf16):
    m, kdim = a.shape
    n = w.shape[1]
    tm = _row_tile(m)
    ntab = tabs[0].shape[0] // tm
    outs, ospecs = [], []
    for want, dt in ((want_f32, F32), (want_bf16, BF16)):
        if want:
            outs.append(jax.ShapeDtypeStruct((m, n), dt))
            ospecs.append(pl.BlockSpec((tm, n), lambda i: (i, 0)))
    tab_spec = pl.BlockSpec((tm, LANES), lambda i: (i % ntab, 0))
    res = pl.pallas_call(
        functools.partial(_proj_kernel, rope=rope, want_f32=want_f32, want_bf16=want_bf16),
        out_shape=tuple(outs),
        grid=(m // tm,),
        in_specs=[pl.BlockSpec((tm, kdim), lambda i: (i, 0)), pl.BlockSpec((kdim, n), lambda i: (0, 0)),
                  tab_spec, tab_spec, tab_spec],
        out_specs=tuple(ospecs),
        compiler_params=_cparams(("parallel",)),
        name="proj_rope" if rope else "proj",
    )(a, w, *tabs)
    return res


def _mm_kernel(a_ref, w_ref, o_ref, *, act):
    acc = _dot(a_ref[...], w_ref[...])
    if act == "sigmoid":
        acc = jax.nn.sigmoid(acc)
    o_ref[...] = acc.astype(o_ref.dtype)


def _mm(a, w, *, tn, out_dtype, act=None):
    m, kdim = a.shape
    n = w.shape[1]
    tm = _row_tile(m)
    return pl.pallas_call(
        functools.partial(_mm_kernel, act=act),
        out_shape=jax.ShapeDtypeStruct((m, n), out_dtype),
        grid=(m // tm, n // tn),
        in_specs=[pl.BlockSpec((tm, kdim), lambda i, j: (i, 0)), pl.BlockSpec((kdim, tn), lambda i, j: (0, j))],
        out_specs=pl.BlockSpec((tm, tn), lambda i, j: (i, j)),
        compiler_params=_cparams(("parallel", "parallel")),
        name="mm_" + (act or "plain"),
    )(a, w)


def _idx_proj_kernel(hh_ref, hl_ref, wh_ref, wl_ref, c_ref, s1_ref, s2_ref,
                     qcat_ref, kcat_ref, kf_ref, wi_ref):
    hh = hh_ref[...]
    wh = wh_ref[...]
    acc = _dot(hh, wh) + _dot(hl_ref[...], wh) + _dot(hh, wl_ref[...])
    c, s1, s2 = c_ref[...], s1_ref[...], s2_ref[...]
    half = IDX_DIM // 8
    zeros = jnp.zeros((acc.shape[0], IDX_DIM), F32)

    def hi_lo(x):
        hi = x.astype(BF16).astype(F32)
        return hi, (x - hi).astype(BF16).astype(F32)

    pieces = []
    for g in range(IDX_HEADS * IDX_DIM // LANES):
        blk = _rope_lanes(acc[:, g * LANES:(g + 1) * LANES], c, s1, s2, half)
        for u in range(LANES // IDX_DIM):
            hi, lo = hi_lo(blk[:, u * IDX_DIM:(u + 1) * IDX_DIM])
            pieces += [hi, lo, hi, zeros]
    qcat_ref[...] = jnp.concatenate(pieces, axis=1).astype(BF16)
    last = acc[:, IDX_HEADS * IDX_DIM:]
    ki = _rope_lanes(last, c, s1, s2, half)[:, :IDX_DIM]
    kf_ref[...] = ki
    hi, lo = hi_lo(ki)
    kcat_ref[...] = jnp.concatenate([hi, hi, lo, zeros], axis=1).astype(BF16)
    wi_ref[...] = last[:, IDX_DIM:IDX_DIM + IDX_HEADS] * (IDX_HEADS ** -0.5 * IDX_DIM ** -0.5)


def _idx_proj(hh, hl, wh, wl, tabs):
    m, d = hh.shape
    tm = _row_tile(m)
    ntab = tabs[0].shape[0] // tm
    row = lambda w: pl.BlockSpec((tm, w), lambda i: (i, 0))
    full = lambda a: pl.BlockSpec(a.shape, lambda i: (0, 0))
    tab_spec = pl.BlockSpec((tm, LANES), lambda i: (i % ntab, 0))
    return pl.pallas_call(
        _idx_proj_kernel,
        out_shape=(jax.ShapeDtypeStruct((m, IDX_HEADS * 256), BF16), jax.ShapeDtypeStruct((m, 256), BF16),
                   jax.ShapeDtypeStruct((m, IDX_DIM), F32), jax.ShapeDtypeStruct((m, IDX_HEADS), F32)),
        grid=(m // tm,),
        in_specs=[row(d), row(d), full(wh), full(wl), tab_spec, tab_spec, tab_spec],
        out_specs=(row(IDX_HEADS * 256), row(256), row(IDX_DIM), row(IDX_HEADS)),
        compiler_params=_cparams(("parallel",)),
        name="idx_proj",
    )(hh, hl, wh, wl, *tabs)


def _sort_key(s):
    bits = lax.bitcast_convert_type(s, I32)
    return bits ^ (lax.shift_right_arithmetic(bits, 31) & 0x7FFFFFFF)


def _topk_mask(s, n_sel, blk):
    r, l = s.shape
    key = _sort_key(s)

    def body(it, prefix):
        cand_u = prefix | lax.shift_left(jnp.int32(1), 31 - it)
        cand_s = cand_u ^ INT_MIN
        cnt = jnp.sum(jnp.where(key >= cand_s, 1.0, 0.0), axis=1, keepdims=True)
        return jnp.where(cnt >= n_sel, cand_u, prefix)

    prefix = lax.fori_loop(0, 32, body, jnp.zeros((r, 1), I32))
    thr = prefix ^ INT_MIN
    gt = key > thr
    eq = key == thr
    need = n_sel - jnp.sum(jnp.where(gt, 1.0, 0.0), axis=1, keepdims=True)
    rr = lax.broadcasted_iota(I32, (blk, blk), 0)
    cc = lax.broadcasted_iota(I32, (blk, blk), 1)
    upper = jnp.where(rr < cc, 1.0, 0.0).astype(BF16)
    offs = jnp.zeros((r, 1), F32)
    parts = []
    for b in range(l // blk):
        e = jnp.where(eq[:, b * blk:(b + 1) * blk], 1.0, 0.0)
        rank = _dot(e.astype(BF16), upper) + offs
        parts.append(jnp.logical_and(eq[:, b * blk:(b + 1) * blk], rank < need))
        offs = offs + jnp.sum(e, axis=1, keepdims=True)
    return jnp.logical_or(gt, jnp.concatenate(parts, axis=1))


def _attn_prompt_kernel(q_ref, k_ref, v_ref, qi_ref, ki_ref, wi_ref, o_ref, *, n_sel):
    tq = q_ref.shape[1]
    s_len = k_ref.shape[1]
    q0 = pl.program_id(1) * tq
    ki = ki_ref[0]
    wi = wi_ref[0]
    score = jnp.zeros((tq, s_len), F32)
    for h in range(IDX_HEADS):
        lg = _dot_nt(qi_ref[0, :, h * 256:(h + 1) * 256], ki)
        score = score + wi[:, h:h + 1] * jnp.maximum(lg, 0.0)
    col = lax.broadcasted_iota(I32, (tq, s_len), 1)
    row = q0 + lax.broadcasted_iota(I32, (tq, s_len), 0)
    causal = col <= row
    sel = _topk_mask(jnp.where(causal, score, NEG), n_sel, RANK_BLK)
    valid = jnp.logical_and(sel, causal)
    for h in range(N_HEADS_A):
        sl = slice(h * HEAD_A, (h + 1) * HEAD_A)
        s = _dot_nt(q_ref[0, :, sl], k_ref[0, :, sl]) * (HEAD_A ** -0.5)
        s = jnp.where(valid, s, NEG)
        m = jnp.max(s, axis=1, keepdims=True)
        p = jnp.where(valid, jnp.exp(s - m), 0.0)
        l = jnp.sum(p, axis=1, keepdims=True)
        o = _dot(p.astype(BF16), v_ref[0, :, sl]) / l
        o_ref[0, :, sl] = o.astype(o_ref.dtype)


def _attn_prompt(q, k, v, qcat, kcat, wi):
    b, s_len, _ = q.shape
    tq = min(TQ, s_len)
    n_sel = min(TOPK, s_len // 4)
    blk_q = lambda w: pl.BlockSpec((1, tq, w), lambda bi, i: (bi, i, 0))
    blk_all = lambda w: pl.BlockSpec((1, s_len, w), lambda bi, i: (bi, 0, 0))
    return pl.pallas_call(
        functools.partial(_attn_prompt_kernel, n_sel=n_sel),
        out_shape=jax.ShapeDtypeStruct((b, s_len, C_A), BF16),
        grid=(b, s_len // tq),
        in_specs=[blk_q(C_A), blk_all(C_A), blk_all(C_A), blk_q(IDX_HEADS * 256), blk_all(256), blk_q(IDX_HEADS)],
        out_specs=blk_q(C_A),
        compiler_params=_cparams(("parallel", "arbitrary")),
        name="attn_prompt",
    )(q, k, v, qcat, kcat, wi)


SEL_PAGES = 8
ATT_PAGES = 4


def _sel_sample_kernel(pt_ref, qh_ref, ql_ref, w_ref, knew_ref, *rest, n_sel, past, t_new):
    pages = rest[:SEL_PAGES]
    mask_ref = rest[SEL_PAGES]
    sc_ref = rest[SEL_PAGES + 1]
    step = pl.program_id(1)
    qh, ql, w = qh_ref[0], ql_ref[0], w_ref[0]
    nq = qh.shape[0] // IDX_HEADS

    def scores_of(kf):
        kh, kl = _split_bf16(kf)
        lg = _dot_nt(qh, kh) + _dot_nt(ql, kh) + _dot_nt(qh, kl)
        x = w * jnp.maximum(lg, 0.0)
        return jnp.sum(x.reshape(nq, IDX_HEADS, PAGE), axis=1)

    for u in range(SEL_PAGES):
        off = pl.multiple_of((step * SEL_PAGES + u) * PAGE, PAGE)
        sc_ref[0:nq, pl.ds(off, PAGE)] = scores_of(pages[u][0])

    @pl.when(step == pl.num_programs(1) - 1)
    def _():
        sc_ref[0:nq, past:past + PAGE] = scores_of(knew_ref[0])
        s = sc_ref[0:nq, :]
        col = lax.broadcasted_iota(I32, s.shape, 1)
        qpos = past + lax.broadcasted_iota(I32, s.shape, 0)
        causal = col <= qpos
        s = jnp.where(causal, s, NEG)
        s = jnp.where(col < past + t_new, s, -jnp.inf)
        sel = _topk_mask(s, n_sel, LANES)
        mask_ref[0] = jnp.where(jnp.logical_and(sel, causal), 1.0, 0.0)


def _sel_sample(page_table, qh, ql, w, knew, cache_ki, t_new):
    b, n_pages = page_table.shape
    past = n_pages * PAGE
    n_sel = min(TOPK, (past + t_new) // 4)
    rows = qh.shape[1]
    nq = rows // IDX_HEADS
    width = past + PAGE
    steps = n_pages // SEL_PAGES
    page_spec = lambda u: pl.BlockSpec((1, PAGE, IDX_DIM), lambda bi, s, pt: (pt[bi, s * SEL_PAGES + u], 0, 0))
    per_b = lambda shp: pl.BlockSpec((1,) + shp, lambda bi, s, pt: (bi, 0, 0))
    return pl.pallas_call(
        functools.partial(_sel_sample_kernel, n_sel=n_sel, past=past, t_new=t_new),
        out_shape=jax.ShapeDtypeStruct((b, nq, width), F32),
        grid_spec=pltpu.PrefetchScalarGridSpec(
            num_scalar_prefetch=1,
            grid=(b, steps),
            in_specs=[per_b((rows, IDX_DIM)), per_b((rows, IDX_DIM)), per_b((rows, 1)), per_b((PAGE, IDX_DIM))]
                     + [page_spec(u) for u in range(SEL_PAGES)],
            out_specs=per_b((nq, width)),
            scratch_shapes=[pltpu.VMEM((8, width), F32)],
        ),
        compiler_params=_cparams(("parallel", "arbitrary")),
        name="sel_sample",
    )(page_table, qh, ql, w, knew, *([cache_ki] * SEL_PAGES))


def _attn_sample_kernel(pt_ref, q_ref, knew_ref, vnew_ref, mnew_ref, mask_ref, *rest, nq):
    kpages = rest[:ATT_PAGES]
    vpages = rest[ATT_PAGES:2 * ATT_PAGES]
    o_ref = rest[2 * ATT_PAGES]
    m_ref, l_ref, acc_ref = rest[2 * ATT_PAGES + 1:]
    step = pl.program_id(1)
    q = q_ref[0]
    rows = q.shape[0]

    def update(kb, vb, mk, first):
        s = _dot_nt(q, kb) * (HEAD_A ** -0.5)
        valid = jnp.broadcast_to(mk[:, None, :], (nq, N_HEADS_A, PAGE)).reshape(rows, PAGE) > 0.5
        s = jnp.where(valid, s, NEG)
        m_blk = jnp.max(s, axis=1, keepdims=True)
        if first:
            m_new = m_blk
            p = jnp.where(valid, jnp.exp(s - m_new), 0.0)
            l_ref[...] = jnp.sum(p, axis=1, keepdims=True)
            acc_ref[...] = _dot(p.astype(BF16), vb)
        else:
            m_old = m_ref[...]
            m_new = jnp.maximum(m_old, m_blk)
            alpha = jnp.exp(m_old - m_new)
            p = jnp.where(valid, jnp.exp(s - m_new), 0.0)
            l_ref[...] = alpha * l_ref[...] + jnp.sum(p, axis=1, keepdims=True)
            acc_ref[...] = alpha * acc_ref[...] + _dot(p.astype(BF16), vb)
        m_ref[...] = m_new

    @pl.when(step == 0)
    def _():
        update(knew_ref[0], vnew_ref[0], mnew_ref[0], True)

    @pl.when(step > 0)
    def _():
        for u in range(ATT_PAGES):
            update(kpages[u][0].astype(BF16), vpages[u][0].astype(BF16),
                   mask_ref[0, :, u * PAGE:(u + 1) * PAGE], False)

    @pl.when(step == pl.num_programs(1) - 1)
    def _():
        out = (acc_ref[...] / l_ref[...]).reshape(nq, N_HEADS_A, C_A)
        for h in range(N_HEADS_A):
            o_ref[0, :, h * HEAD_A:(h + 1) * HEAD_A] = out[:, h, h * HEAD_A:(h + 1) * HEAD_A]


def _attn_sample(page_table, qbd, knew, vnew, mask, cache_k, cache_v):
    b, n_pages = page_table.shape
    rows = qbd.shape[1]
    nq = rows // N_HEADS_A
    past = n_pages * PAGE
    steps = n_pages // ATT_PAGES + 1
    pg = lambda s, u: jnp.maximum(s - 1, 0) * ATT_PAGES + u
    page_spec = lambda u: pl.BlockSpec((1, PAGE, C_A), lambda bi, s, pt: (pt[bi, pg(s, u)], 0, 0))
    per_b = lambda shp: pl.BlockSpec((1,) + shp, lambda bi, s, pt: (bi, 0, 0))
    return pl.pallas_call(
        functools.partial(_attn_sample_kernel, nq=nq),
        out_shape=jax.ShapeDtypeStruct((b, nq, C_A), F32),
        grid_spec=pltpu.PrefetchScalarGridSpec(
            num_scalar_prefetch=1,
            grid=(b, steps),
            in_specs=[per_b((rows, C_A)), per_b((PAGE, C_A)), per_b((PAGE, C_A)),
                      pl.BlockSpec((1, nq, PAGE), lambda bi, s, pt: (bi, 0, past // PAGE)),
                      pl.BlockSpec((1, nq, ATT_PAGES * PAGE), lambda bi, s, pt: (bi, 0, jnp.maximum(s - 1, 0)))]
                     + [page_spec(u) for u in range(ATT_PAGES)] * 2,
            out_specs=per_b((nq, C_A)),
            scratch_shapes=[pltpu.VMEM((rows, 1), F32), pltpu.VMEM((rows, 1), F32), pltpu.VMEM((rows, C_A), F32)],
        ),
        compiler_params=_cparams(("parallel", "arbitrary")),
        name="attn_sample",
    )(page_table, qbd, knew, vnew, mask, mask, *([cache_k] * ATT_PAGES), *([cache_v] * ATT_PAGES))


def _seg_sum(x, jmat):
    c = x.shape[0]
    xs = jnp.concatenate([x[:, p * LANES:(p + 1) * LANES] for p in range(N_PAIRS)], axis=0)
    hi, lo = _split_bf16(xs)
    ys = _dot(hi, jmat) + _dot(lo, jmat)
    return jnp.concatenate([ys[p * c:(p + 1) * c] for p in range(N_PAIRS)], axis=1)


def _rwkv_kernel(pb_ref, first_ref, s0_ref, mu_ref, w0_ref, a0_ref, kk_ref, ka_ref, rk_ref, lw_ref, lb_ref,
                 w2h, w2l, a2h, a2l, g2h, g2l, ob_ref, st_ref, carry_ref, state_ref, *, t_valid):
    c = pb_ref.shape[1]
    n = pl.program_id(1)
    half = HEAD_B

    @pl.when(n == 0)
    def _():
        carry_ref[...] = jnp.broadcast_to(first_ref[0], carry_ref.shape)
        z = jnp.zeros((half, half), F32)
        for p in range(N_PAIRS):
            top = jnp.concatenate([s0_ref[0, 2 * p], z], axis=1)
            bot = jnp.concatenate([z, s0_ref[0, 2 * p + 1]], axis=1)
            state_ref[p] = jnp.concatenate([top, bot], axis=0)

    pb = pb_ref[0]
    rowi = lax.broadcasted_iota(I32, (c, 1), 0)
    prev = jnp.where(rowi == 0, carry_ref[0:1, :], pltpu.roll(pb, 1, axis=0))
    carry_ref[0:1, :] = pb[c - 1:c, :]
    xm = pb + (prev - pb) * mu_ref[...]

    r = xm[:, 0:C_B]
    k = xm[:, C_B:2 * C_B]
    v = xm[:, 2 * C_B:3 * C_B]
    xw = xm[:, 3 * C_B:3 * C_B + 128]
    xa = xm[:, 3 * C_B + 128:3 * C_B + 256]
    xg = xm[:, 3 * C_B + 256:3 * C_B + 512]
    w = -jax.nn.softplus(-(w0_ref[...] + _dot3w(jnp.tanh(xw), w2h, w2l))) - 0.5
    ld = -jnp.exp(w)
    a = jax.nn.sigmoid(a0_ref[...] + _dot3w(xa, a2h, a2l))
    g = _dot3w(jax.nn.sigmoid(xg), g2h, g2l)

    li = lax.broadcasted_iota(I32, (LANES, LANES), 0)
    lj = lax.broadcasted_iota(I32, (LANES, LANES), 1)
    same_head = (li // half) == (lj // half)
    jmat = jnp.where(same_head, 1.0, 0.0).astype(BF16)

    kkr = k * kk_ref[...]
    kk = kkr / jnp.maximum(jnp.sqrt(_seg_sum(kkr * kkr, jmat)), 1e-12)
    k2 = k * (1.0 + (a - 1.0) * ka_ref[...])
    bonus = _seg_sum(r * k2 * rk_ref[...], jmat)
    avec = -kk
    bvec = kk * a
    if t_valid < c:
        live = rowi < t_valid
        ld = jnp.where(live, ld, 0.0)
        avec = jnp.where(live, avec, 0.0)
        bvec = jnp.where(live, bvec, 0.0)
        k2s = jnp.where(live, k2, 0.0)
        vs = jnp.where(live, v, 0.0)
    else:
        k2s, vs = k2, v

    ti = lax.broadcasted_iota(I32, (c, c), 0)
    tj = lax.broadcasted_iota(I32, (c, c), 1)
    tri = jnp.where(tj <= ti, 1.0, 0.0).astype(BF16)
    l1 = ld.astype(BF16)
    rem = ld - l1.astype(F32)
    l2 = rem.astype(BF16)
    l3 = (rem - l2.astype(F32)).astype(BF16)
    logc = _dot(tri, l1) + _dot(tri, l2) + _dot(tri, l3)
    logc_end = logc[c - 1:c, :]
    e_pos = jnp.exp(logc)
    e_neg = jnp.exp(-logc)
    e_end = jnp.exp(logc_end - logc)
    rt = r * e_pos
    at = avec * jnp.exp(logc - ld)
    kt = k2s * e_neg
    bt = bvec * e_neg
    kc = k2s * e_end
    bc = bvec * e_end
    dec_end = jnp.exp(logc_end)

    lane = lax.broadcasted_iota(I32, (1, LANES), 1)
    m0 = lane < half
    col4 = lax.broadcasted_iota(I32, (c, 4 * c), 1) % c
    row4 = lax.broadcasted_iota(I32, (c, 4 * c), 0)
    eye2 = jnp.where(lax.broadcasted_iota(I32, (2 * c, 2 * c), 0) == lax.broadcasted_iota(I32, (2 * c, 2 * c), 1),
                     1.0, 0.0)
    lane2c = lax.broadcasted_iota(I32, (1, 2 * c), 1) < c
    n_dbl = int(np.log2(c)) - 1

    def two(x):
        return jnp.concatenate([jnp.where(m0, x, 0.0), jnp.where(m0, 0.0, x)], axis=0)

    ys = []
    for p in range(N_PAIRS):
        sl = slice(p * LANES, (p + 1) * LANES)
        s_old = state_ref[p]
        lhs1 = jnp.concatenate([rt[:, sl], at[:, sl]], axis=0).astype(BF16)
        rhs1 = jnp.concatenate([two(kt[:, sl]), two(bt[:, sl])], axis=0).astype(BF16)
        amat = _dot_nt(lhs1, rhs1)
        a_r = jnp.where(col4 <= row4, amat[:c], 0.0)
        a_a = jnp.where(col4 < row4, amat[c:], 0.0)
        x0 = _dot_nt(lhs1, s_old.astype(BF16))
        vm = two(vs[:, sl]).astype(BF16)
        x = x0[c:] + _dot(a_a[:, :2 * c].astype(BF16), vm)
        lab = a_a[:, 2 * c:]
        lbd = jnp.concatenate([jnp.where(lane2c, lab, 0.0), jnp.where(lane2c, 0.0, lab)], axis=0)
        pm = eye2 + lbd
        lk = lbd
        for _ in range(n_dbl):
            lkb = lk.astype(BF16)
            lk = _dot(lkb, lkb)
            pm = pm + _dot(pm.astype(BF16), lk.astype(BF16))
        us = _dot(pm.astype(BF16), two(x).astype(BF16))
        u = us[:c] + us[c:]
        um = two(u).astype(BF16)
        y = x0[:c] + _dot(a_r.astype(BF16), jnp.concatenate([vm, um], axis=0))
        ys.append(y)
        vals = jnp.concatenate([vs[:, sl], u], axis=0)
        keys = jnp.concatenate([kc[:, sl], bc[:, sl]], axis=0).astype(BF16)
        add = _dot(vals.T.astype(BF16), keys)
        state_ref[p] = s_old * dec_end[:, sl] + jnp.where(same_head, add, 0.0)

    y = jnp.concatenate(ys, axis=1)
    mean = _seg_sum(y, jmat) * (1.0 / half)
    d = y - mean
    var = _seg_sum(d * d, jmat) * (1.0 / half)
    yn = d * lax.rsqrt(var + LNX_EPS) * lw_ref[...] + lb_ref[...]
    ob_ref[0] = ((yn + bonus * v) * g).astype(ob_ref.dtype)

    @pl.when(n == pl.num_programs(1) - 1)
    def _():
        for p in range(N_PAIRS):
            s_new = state_ref[p]
            st_ref[0, 2 * p] = s_new[:half, :half]
            st_ref[0, 2 * p + 1] = s_new[half:, half:]


def _rwkv(pb, first, s0, vecs, loras, t_valid):
    b, t, _ = pb.shape
    vec = lambda a: pl.BlockSpec(a.shape, lambda bi, n: (0, 0))
    return pl.pallas_call(
        functools.partial(_rwkv_kernel, t_valid=t_valid),
        out_shape=(jax.ShapeDtypeStruct((b, t, C_B), BF16), jax.ShapeDtypeStruct(s0.shape, F32)),
        grid=(b, t // CHUNK),
        in_specs=[pl.BlockSpec((1, CHUNK, PB_W), lambda bi, n: (bi, n, 0)),
                  pl.BlockSpec((1, 1, PB_W), lambda bi, n: (bi, 0, 0)),
                  pl.BlockSpec((1,) + s0.shape[1:], lambda bi, n: (bi, 0, 0, 0))]
                 + [vec(a) for a in vecs] + [vec(a) for a in loras],
        out_specs=(pl.BlockSpec((1, CHUNK, C_B), lambda bi, n: (bi, n, 0)),
                   pl.BlockSpec((1,) + s0.shape[1:], lambda bi, n: (bi, 0, 0, 0))),
        scratch_shapes=[pltpu.VMEM((8, PB_W), F32), pltpu.VMEM((N_PAIRS, LANES, LANES), F32)],
        compiler_params=_cparams(("parallel", "arbitrary")),
        name="rwkv7",
    )(pb, first, s0, *vecs, *loras)


def _merge_kernel(oa_ref, ob_ref, wpa_ref, wpb_ref, ga_ref, gb_ref, o_ref):
    za = _dot(oa_ref[...], wpa_ref[...])
    zb = _dot(ob_ref[...], wpb_ref[...])
    o_ref[...] = (ga_ref[...].astype(F32) * za + gb_ref[...].astype(F32) * zb).astype(o_ref.dtype)


def _merge(oa, ob, wpa, wpb, gates, tn=1024):
    m = oa.shape[0]
    tm = _row_tile(m)
    nb = D_MODEL // tn
    return pl.pallas_call(
        _merge_kernel,
        out_shape=jax.ShapeDtypeStruct((m, D_MODEL), BF16),
        grid=(m // tm, nb),
        in_specs=[pl.BlockSpec((tm, C_A), lambda i, j: (i, 0)), pl.BlockSpec((tm, C_B), lambda i, j: (i, 0)),
                  pl.BlockSpec((C_A, tn), lambda i, j: (0, j)), pl.BlockSpec((C_B, tn), lambda i, j: (0, j)),
                  pl.BlockSpec((tm, tn), lambda i, j: (i, j)), pl.BlockSpec((tm, tn), lambda i, j: (i, j + nb))],
        out_specs=pl.BlockSpec((tm, tn), lambda i, j: (i, j)),
        compiler_params=_cparams(("parallel", "parallel")),
        name="merge",
    )(oa, ob, wpa, wpb, gates, gates)


def _oproj_kernel(x_ref, mix_ref, wo_ref, g_ref, x1_ref, h2_ref):
    x1 = x_ref[...] + _dot(mix_ref[...], wo_ref[...])
    x1_ref[...] = x1
    h2 = x1 * lax.rsqrt(jnp.mean(x1 * x1, axis=-1, keepdims=True) + NORM_EPS) * g_ref[...]
    h2_ref[...] = h2.astype(BF16)


def _oproj(x, mixed, wo, g):
    m, d = x.shape
    tm = _row_tile(m)
    row = pl.BlockSpec((tm, d), lambda i: (i, 0))
    return pl.pallas_call(
        _oproj_kernel,
        out_shape=(jax.ShapeDtypeStruct((m, d), F32), jax.ShapeDtypeStruct((m, d), BF16)),
        grid=(m // tm,),
        in_specs=[row, row, pl.BlockSpec((d, d), lambda i: (0, 0)), pl.BlockSpec((1, d), lambda i: (0, 0))],
        out_specs=(row, row),
        compiler_params=_cparams(("parallel",)),
        name="oproj_norm",
    )(x, mixed, wo, g.reshape(1, d))


def _ffn_up_kernel(h_ref, wg_ref, wu_ref, o_ref):
    h = h_ref[...]
    gate = _dot(h, wg_ref[...])
    up = _dot(h, wu_ref[...])
    o_ref[...] = (gate * jax.nn.sigmoid(gate) * up).astype(o_ref.dtype)


def _ffn_up(h2, wg, wu, tn=512):
    m, d = h2.shape
    n = wg.shape[1]
    tm = _row_tile(m)
    return pl.pallas_call(
        _ffn_up_kernel,
        out_shape=jax.ShapeDtypeStruct((m, n), BF16),
        grid=(m // tm, n // tn),
        in_specs=[pl.BlockSpec((tm, d), lambda i, j: (i, 0)), pl.BlockSpec((d, tn), lambda i, j: (0, j)),
                  pl.BlockSpec((d, tn), lambda i, j: (0, j))],
        out_specs=pl.BlockSpec((tm, tn), lambda i, j: (i, j)),
        compiler_params=_cparams(("parallel", "parallel")),
        name="ffn_up",
    )(h2, wg, wu)


def _ffn_down_kernel(x_ref, a_ref, wd_ref, o_ref):
    o_ref[...] = x_ref[...] + _dot(a_ref[...], wd_ref[...])


def _ffn_down(x1, act, wd, tn=1024):
    m, d = x1.shape
    kdim = act.shape[1]
    tm = _row_tile(m)
    return pl.pallas_call(
        _ffn_down_kernel,
        out_shape=jax.ShapeDtypeStruct((m, d), F32),
        grid=(m // tm, d // tn),
        in_specs=[pl.BlockSpec((tm, tn), lambda i, j: (i, j)), pl.BlockSpec((tm, kdim), lambda i, j: (i, 0)),
                  pl.BlockSpec((kdim, tn), lambda i, j: (0, j))],
        out_specs=pl.BlockSpec((tm, tn), lambda i, j: (i, j)),
        compiler_params=_cparams(("parallel", "parallel")),
        name="ffn_down",
    )(x1, act, wd)


def _pad_cols(w, width):
    return jnp.pad(w, ((0, 0), (0, width - w.shape[1])))


def _pad_rows(w, height):
    return jnp.pad(w, ((0, height - w.shape[0]), (0, 0)))


def _prep_weights(w_in, mu_b, w2, a2, g2):
    o = 0
    cols = {}
    for name, wdt in (("q", C_A), ("k", C_A), ("v", C_A), ("qi", IDX_HEADS * IDX_DIM), ("ki", IDX_DIM),
                      ("wi", IDX_HEADS), ("g", 2 * D_MODEL), ("rkv", 3 * C_B), ("xw", D_DECAY), ("xa", D_AAA),
                      ("xg", D_GATE)):
        cols[name] = (o, o + wdt)
        o += wdt
    cut = lambda name: w_in[:, cols[name][0]:cols[name][1]]
    w_idx = _pad_cols(jnp.concatenate([cut("qi"), cut("ki"), cut("wi")], axis=1), IDX_W)
    idx_hi, idx_lo = _split_bf16(w_idx)
    w_b = jnp.concatenate([cut("rkv"), _pad_cols(cut("xw"), 128), _pad_cols(cut("xa"), 128),
                           _pad_cols(cut("xg"), 256)], axis=1)
    nb0 = cols["rkv"][0]
    mu = lambda name: mu_b[cols[name][0] - nb0:cols[name][1] - nb0][None, :]
    mu_p = jnp.concatenate([mu("rkv"), _pad_cols(mu("xw"), 128), _pad_cols(mu("xa"), 128),
                            _pad_cols(mu("xg"), 256)], axis=1)
    loras = []
    for wl, hgt in ((w2, 128), (a2, 128), (g2, 256)):
        loras += list(_split_bf16(_pad_rows(wl, hgt)))
    return dict(q=cut("q").astype(BF16), k=cut("k").astype(BF16), v=cut("v").astype(BF16),
                idx_hi=idx_hi, idx_lo=idx_lo, g=cut("g").astype(BF16), b=w_b.astype(BF16), mu=mu_p,
                loras=loras)


def _layer(x, pos, wts, p, *, seq, attn_fn, first_pb, s0, t_valid):
    m = x.shape[0]
    nb = m // seq
    hh, hl = _norm_split(x, p["g_mix"])
    tabs_a = _rope_tables(pos, HEAD_A)
    tabs_i = _rope_tables(pos, IDX_DIM)
    (q_b,) = _proj(hh, wts["q"], tabs_a, rope=True, want_f32=False, want_bf16=True)
    k_f, k_b = _proj(hh, wts["k"], tabs_a, rope=True, want_f32=True, want_bf16=True)
    v_f, v_b = _proj(hh, wts["v"], tabs_a, rope=False, want_f32=True, want_bf16=True)
    qcat, kcat, ki_f, wi = _idx_proj(hh, hl, wts["idx_hi"], wts["idx_lo"], tabs_i)
    gates = _mm(hh, wts["g"], tn=1024, out_dtype=BF16, act="sigmoid")
    pb = _mm(hh, wts["b"], tn=PB_W // 4, out_dtype=F32)

    o_a = attn_fn(q_b, k_b, v_b, qcat, kcat, ki_f, wi)

    t_pad = -(-seq // CHUNK) * CHUNK
    pb3 = pb.reshape(nb, seq, PB_W)
    if t_pad != seq:
        pb3 = jnp.pad(pb3, ((0, 0), (0, t_pad - seq), (0, 0)))
    vecs = [wts["mu"]] + [p[n].reshape(1, C_B) for n in ("w0", "a0", "k_k", "k_a", "r_k", "lnx_w", "lnx_b")]
    o_b, wkv = _rwkv(pb3, first_pb, s0, vecs, wts["loras"], t_valid)
    o_b = o_b[:, :seq].reshape(m, C_B)

    mixed = _merge(o_a, o_b, p["w_pa"], p["w_pb"], gates)
    x1, h2 = _oproj(x, mixed, p["w_o"], p["g_ffn"])
    act = _ffn_up(h2, p["w_gate"], p["w_up"])
    x2 = _ffn_down(x1, act, p["w_down"])
    shift = _norm_f32(x.reshape(nb, seq, D_MODEL)[:, -1], p["g_mix"])
    return x2, k_f, v_f, ki_f, wkv, shift


def kernel(x_prompt, x_sample, cache_k, cache_v, cache_kidx, state_wkv, state_shift, page_table, g_mix, w_in, mu_b, w0, w2, a0, a2, g2, k_k, k_a, r_k, lnx_w, lnx_b, w_pa, w_pb, w_o, g_ffn, w_gate, w_up, w_down, g_final):
    bp, sp, d = x_prompt.shape
    bs, ts, _ = x_sample.shape
    depth = w_in.shape[0]
    n_pages = page_table.shape[1]
    past = n_pages * PAGE
    xp = x_prompt.reshape(bp * sp, d)
    xs = x_sample.reshape(bs * ts, d)
    outs_p, outs_s = [], []
    for l in range(depth):
        wts = _prep_weights(w_in[l], mu_b[l], w2[l], a2[l], g2[l])
        p = dict(g_mix=g_mix[l], w0=w0[l], a0=a0[l], k_k=k_k[l], k_a=k_a[l], r_k=r_k[l], lnx_w=lnx_w[l],
                 lnx_b=lnx_b[l], w_pa=w_pa[l].astype(BF16), w_pb=w_pb[l].astype(BF16), w_o=w_o[l].astype(BF16),
                 g_ffn=g_ffn[l], w_gate=w_gate[l].astype(BF16), w_up=w_up[l].astype(BF16),
                 w_down=w_down[l].astype(BF16))

        def attn_p(q_b, k_b, v_b, qcat, kcat, ki_f, wi):
            r3 = lambda a: a.reshape(bp, sp, a.shape[-1])
            return _attn_prompt(r3(q_b), r3(k_b), r3(v_b), r3(qcat), r3(kcat), r3(wi)).reshape(bp * sp, C_A)

        def attn_s(q_b, k_b, v_b, qcat, kcat, ki_f, wi):
            qc = qcat.reshape(bs, ts * IDX_HEADS, 256)
            qh, ql = qc[..., :IDX_DIM], qc[..., IDX_DIM:2 * IDX_DIM]
            wrow = wi.reshape(bs, ts * IDX_HEADS, 1)
            pad_t = lambda a: jnp.pad(a.reshape(bs, ts, a.shape[-1]), ((0, 0), (0, PAGE - ts), (0, 0)))
            mask = _sel_sample(page_table, qh, ql, wrow, pad_t(ki_f),
                               cache_kidx[l].reshape(-1, PAGE, IDX_DIM), ts)
            eye = jnp.eye(N_HEADS_A, dtype=BF16)
            q4 = q_b.reshape(bs, ts, N_HEADS_A, 1, HEAD_A)
            qbd = (q4 * eye[None, None, :, :, None]).reshape(bs, ts * N_HEADS_A, C_A)
            o = _attn_sample(page_table, qbd, pad_t(k_b), pad_t(v_b), mask,
                             cache_k[l].reshape(-1, PAGE, C_A), cache_v[l].reshape(-1, PAGE, C_A))
            return o.reshape(bs * ts, C_A).astype(BF16)

        zero_first = jnp.zeros((bp, 1, PB_W), F32)
        zero_state = jnp.zeros((bp, N_HEADS_B, HEAD_B, HEAD_B), F32)
        xp, kp, vp, kip, wkvp, shp = _layer(xp, jnp.arange(sp), wts, p, seq=sp, attn_fn=attn_p,
                                            first_pb=zero_first, s0=zero_state, t_valid=CHUNK)
        first_s = _mm(state_shift[l].astype(BF16), wts["b"], tn=PB_W // 4, out_dtype=F32).reshape(bs, 1, PB_W)
        pos_s = jnp.tile(past + jnp.arange(ts), bs)
        xs, ksm, vsm, kis, wkvs, shs = _layer(xs, pos_s, wts, p, seq=ts, attn_fn=attn_s,
                                              first_pb=first_s, s0=state_wkv[l], t_valid=ts)
        outs_p.append((kp.reshape(bp, sp, N_HEADS_A, HEAD_A), vp.reshape(bp, sp, N_HEADS_A, HEAD_A),
                       kip.reshape(bp, sp, IDX_DIM), wkvp, shp))
        outs_s.append((ksm.reshape(bs, ts, N_HEADS_A, HEAD_A), vsm.reshape(bs, ts, N_HEADS_A, HEAD_A),
                       kis.reshape(bs, ts, IDX_DIM), wkvs, shs))
    y_prompt = _norm_f32(xp, g_final).reshape(bp, sp, d)
    y_sample = _norm_f32(xs, g_final).reshape(bs, ts, d)
    stack = lambda outs, i: jnp.stack([o[i] for o in outs])
    return ((y_prompt, y_sample) + tuple(stack(outs_p, i) for i in range(5))
            + tuple(stack(outs_s, i) for i in range(5)))
```

```python
import functools

import numpy as np
import jax
import jax.numpy as jnp
from jax import lax
from jax.experimental import pallas as pl
from jax.experimental.pallas import tpu as pltpu

F32 = jnp.float32
BF16 = jnp.bfloat16
I32 = jnp.int32

D_MODEL = 2048
HEAD_A = 128
N_HEADS_A = 8
C_A = N_HEADS_A * HEAD_A
IDX_HEADS = 8
IDX_DIM = 64
TOPK = 256
HEAD_B = 64
N_HEADS_B = 16
C_B = N_HEADS_B * HEAD_B
D_DECAY = 64
D_AAA = 64
D_GATE = 160
LNX_EPS = 64e-5
ROPE_THETA = 500000.0
NORM_EPS = 1e-6
PAGE = 128
NEG = -1e30
INT_MIN = -(2 ** 31)

LANES = 128
VMEM_LIMIT = 56 * 1024 * 1024
CHUNK = 64
RWKV_NB = 4
N_PAIRS = N_HEADS_B // 2
TQ = 256
RANK_BLK = 256
MASK_ROWS = 16
PB_W = 3 * C_B + 128 + 128 + 256
IDX_W = 640


def _cparams(sem):
    return pltpu.CompilerParams(dimension_semantics=sem, vmem_limit_bytes=VMEM_LIMIT)


def _split_bf16(x):
    hi = x.astype(BF16)
    lo = (x - hi.astype(F32)).astype(BF16)
    return hi, lo


def _dot(a, b):
    return jnp.dot(a, b, preferred_element_type=F32)


def _dot_nt(a, b):
    return lax.dot_general(a, b, (((1,), (1,)), ((), ())), preferred_element_type=F32)


def _bmm(a, b):
    return lax.dot_general(a, b, (((2,), (1,)), ((0,), (0,))), preferred_element_type=F32)


def _bmm_nt(a, b):
    return lax.dot_general(a, b, (((2,), (2,)), ((0,), (0,))), preferred_element_type=F32)


def _dot3w(x, wh_ref, wl_ref):
    xh, xl = _split_bf16(x)
    wh = wh_ref[...]
    return _dot(xh, wh) + _dot(xl, wh) + _dot(xh, wl_ref[...])


def _norm_split_kernel(x_ref, g_ref, hi_ref, lo_ref):
    x = x_ref[...]
    y = x * lax.rsqrt(jnp.mean(x * x, axis=-1, keepdims=True) + NORM_EPS) * g_ref[...]
    hi = y.astype(BF16)
    hi_ref[...] = hi
    lo_ref[...] = (y - hi.astype(F32)).astype(BF16)


def _norm_f32_kernel(x_ref, g_ref, o_ref):
    x = x_ref[...]
    o_ref[...] = x * lax.rsqrt(jnp.mean(x * x, axis=-1, keepdims=True) + NORM_EPS) * g_ref[...]


def _row_tile(m):
    return 512 if m % 512 == 0 else m


def _norm_split(x, g):
    m, d = x.shape
    tm = _row_tile(m)
    return pl.pallas_call(
        _norm_split_kernel,
        out_shape=(jax.ShapeDtypeStruct((m, d), BF16), jax.ShapeDtypeStruct((m, d), BF16)),
        grid=(m // tm,),
        in_specs=[pl.BlockSpec((tm, d), lambda i: (i, 0)), pl.BlockSpec((1, d), lambda i: (0, 0))],
        out_specs=(pl.BlockSpec((tm, d), lambda i: (i, 0)), pl.BlockSpec((tm, d), lambda i: (i, 0))),
        compiler_params=_cparams(("parallel",)),
        name="norm_split",
    )(x, g.reshape(1, d))


def _norm_f32(x, g):
    m, d = x.shape
    tm = _row_tile(m)
    return pl.pallas_call(
        _norm_f32_kernel,
        out_shape=jax.ShapeDtypeStruct((m, d), F32),
        grid=(m // tm,),
        in_specs=[pl.BlockSpec((tm, d), lambda i: (i, 0)), pl.BlockSpec((1, d), lambda i: (0, 0))],
        out_specs=pl.BlockSpec((tm, d), lambda i: (i, 0)),
        compiler_params=_cparams(("parallel",)),
        name="norm_f32",
    )(x, g.reshape(1, d))


def _rope_tables(pos, head_dim):
    rot = head_dim // 4
    half = rot // 2
    inv = ROPE_THETA ** (-jnp.arange(half, dtype=F32) * 2.0 / rot)
    ang = pos.astype(F32)[:, None] * inv[None, :]
    cos, sin = jnp.cos(ang), jnp.sin(ang)
    t = pos.shape[0]
    c = jnp.ones((t, head_dim), F32).at[:, :half].set(cos).at[:, half:rot].set(cos)
    s1 = jnp.zeros((t, head_dim), F32).at[:, half:rot].set(sin)
    s2 = jnp.zeros((t, head_dim), F32).at[:, :half].set(-sin)
    rep = LANES // head_dim
    return tuple(jnp.tile(a, (1, rep)) for a in (c, s1, s2))


def _rope_lanes(x, c, s1, s2, half):
    return x * c + pltpu.roll(x, half, axis=1) * s1 + pltpu.roll(x, LANES - half, axis=1) * s2


def _proj_kernel(a_ref, w_ref, c_ref, s1_ref, s2_ref, *out_refs, rope, want_f32, want_bf16, scale):
    acc = _dot(a_ref[...], w_ref[...])
    if rope:
        c, s1, s2 = c_ref[...], s1_ref[...], s2_ref[...]
        parts = [_rope_lanes(acc[:, h * LANES:(h + 1) * LANES], c, s1, s2, HEAD_A // 8)
                 for h in range(acc.shape[1] // LANES)]
        acc = jnp.concatenate(parts, axis=1)
    if scale is not None:
        acc = acc * scale
    k = 0
    if want_f32:
        out_refs[k][...] = acc
        k += 1
    if want_bf16:
        out_refs[k][...] = acc.astype(BF16)


def _proj(a, w, tabs, *, rope, want_f32, want_bf16, scale=None):
    m, kdim = a.shape
    n = w.shape[1]
    tm = _row_tile(m)
    ntab = tabs[0].shape[0] // tm
    outs, ospecs = [], []
    for want, dt in ((want_f32, F32), (want_bf16, BF16)):
        if want:
            outs.append(jax.ShapeDtypeStruct((m, n), dt))
            ospecs.append(pl.BlockSpec((tm, n), lambda i: (i, 0)))
    tab_spec = pl.BlockSpec((tm, LANES), lambda i: (i % ntab, 0))
    res = pl.pallas_call(
        functools.partial(_proj_kernel, rope=rope, want_f32=want_f32, want_bf16=want_bf16, scale=scale),
        out_shape=tuple(outs),
        grid=(m // tm,),
        in_specs=[pl.BlockSpec((tm, kdim), lambda i: (i, 0)), pl.BlockSpec((kdim, n), lambda i: (0, 0)),
                  tab_spec, tab_spec, tab_spec],
        out_specs=tuple(ospecs),
        compiler_params=_cparams(("parallel",)),
        name="proj_rope" if rope else "proj",
    )(a, w, *tabs)
    return res


def _mm_kernel(a_ref, w_ref, o_ref, *, act):
    acc = _dot(a_ref[...], w_ref[...])
    if act == "sigmoid":
        acc = jax.nn.sigmoid(acc)
    o_ref[...] = acc.astype(o_ref.dtype)


def _mm(a, w, *, tn, out_dtype, act=None):
    m, kdim = a.shape
    n = w.shape[1]
    tm = _row_tile(m)
    return pl.pallas_call(
        functools.partial(_mm_kernel, act=act),
        out_shape=jax.ShapeDtypeStruct((m, n), out_dtype),
        grid=(m // tm, n // tn),
        in_specs=[pl.BlockSpec((tm, kdim), lambda i, j: (i, 0)), pl.BlockSpec((kdim, tn), lambda i, j: (0, j))],
        out_specs=pl.BlockSpec((tm, tn), lambda i, j: (i, j)),
        compiler_params=_cparams(("parallel", "parallel")),
        name="mm_" + (act or "plain"),
    )(a, w)


def _idx_proj_kernel(hh_ref, hl_ref, wh_ref, wl_ref, c_ref, s1_ref, s2_ref,
                     qcat_ref, kcat_ref, kf_ref, wi_ref):
    hh = hh_ref[...]
    wh = wh_ref[...]
    acc = _dot(hh, wh) + _dot(hl_ref[...], wh) + _dot(hh, wl_ref[...])
    c, s1, s2 = c_ref[...], s1_ref[...], s2_ref[...]
    half = IDX_DIM // 8
    zeros = jnp.zeros((acc.shape[0], IDX_DIM), F32)

    def hi_lo(x):
        hi = x.astype(BF16).astype(F32)
        return hi, (x - hi).astype(BF16).astype(F32)

    pieces = []
    for g in range(IDX_HEADS * IDX_DIM // LANES):
        blk = _rope_lanes(acc[:, g * LANES:(g + 1) * LANES], c, s1, s2, half)
        for u in range(LANES // IDX_DIM):
            hi, lo = hi_lo(blk[:, u * IDX_DIM:(u + 1) * IDX_DIM])
            pieces += [hi, lo, hi, zeros]
    qcat_ref[...] = jnp.concatenate(pieces, axis=1).astype(BF16)
    last = acc[:, IDX_HEADS * IDX_DIM:]
    ki = _rope_lanes(last, c, s1, s2, half)[:, :IDX_DIM]
    kf_ref[...] = ki
    hi, lo = hi_lo(ki)
    kcat_ref[...] = jnp.concatenate([hi, hi, lo, zeros], axis=1).astype(BF16)
    wi_ref[...] = last[:, IDX_DIM:IDX_DIM + IDX_HEADS] * (IDX_HEADS ** -0.5 * IDX_DIM ** -0.5)


def _idx_proj(hh, hl, wh, wl, tabs):
    m, d = hh.shape
    tm = _row_tile(m)
    ntab = tabs[0].shape[0] // tm
    row = lambda w: pl.BlockSpec((tm, w), lambda i: (i, 0))
    full = lambda a: pl.BlockSpec(a.shape, lambda i: (0, 0))
    tab_spec = pl.BlockSpec((tm, LANES), lambda i: (i % ntab, 0))
    return pl.pallas_call(
        _idx_proj_kernel,
        out_shape=(jax.ShapeDtypeStruct((m, IDX_HEADS * 256), BF16), jax.ShapeDtypeStruct((m, 256), BF16),
                   jax.ShapeDtypeStruct((m, IDX_DIM), F32), jax.ShapeDtypeStruct((m, IDX_HEADS), F32)),
        grid=(m // tm,),
        in_specs=[row(d), row(d), full(wh), full(wl), tab_spec, tab_spec, tab_spec],
        out_specs=(row(IDX_HEADS * 256), row(256), row(IDX_DIM), row(IDX_HEADS)),
        compiler_params=_cparams(("parallel",)),
        name="idx_proj",
    )(hh, hl, wh, wl, *tabs)


def _sort_key(s):
    bits = lax.bitcast_convert_type(s, I32)
    return bits ^ (lax.shift_right_arithmetic(bits, 31) & 0x7FFFFFFF)


def _kth_largest_key(key, n_sel):
    r = key.shape[0]

    def body(it, prefix):
        cand_u = prefix | lax.shift_left(jnp.int32(1), 31 - it)
        cand_s = cand_u ^ INT_MIN
        cnt = jnp.sum(jnp.where(key >= cand_s, 1.0, 0.0), axis=1, keepdims=True)
        return jnp.where(cnt >= n_sel, cand_u, prefix)

    return lax.fori_loop(0, 32, body, jnp.zeros((r, 1), I32)) ^ INT_MIN


def _tie_select(key, thr, n_sel, blk):
    r, l = key.shape
    gt = key > thr
    eq = key == thr
    need = n_sel - jnp.sum(jnp.where(gt, 1.0, 0.0), axis=1, keepdims=True)
    rr = lax.broadcasted_iota(I32, (blk, blk), 0)
    cc = lax.broadcasted_iota(I32, (blk, blk), 1)
    upper = jnp.where(rr < cc, 1.0, 0.0).astype(BF16)
    offs = jnp.zeros((r, 1), F32)
    parts = []
    for b in range(l // blk):
        e = jnp.where(eq[:, b * blk:(b + 1) * blk], 1.0, 0.0)
        rank = _dot(e.astype(BF16), upper) + offs
        parts.append(jnp.logical_and(eq[:, b * blk:(b + 1) * blk], rank < need))
        offs = offs + jnp.sum(e, axis=1, keepdims=True)
    return jnp.logical_or(gt, jnp.concatenate(parts, axis=1))


def _selection_bias(s, causal, n_sel, blk, bias_ref):
    key = _sort_key(s)
    thr = _kth_largest_key(key, n_sel)
    ge_c = jnp.logical_and(key >= thr, causal)
    cnt = (jnp.sum(jnp.where(ge_c, 1.0, 0.0), axis=1, keepdims=True)
           + jnp.sum(jnp.where(jnp.logical_or(key <= thr, causal), 0.0, 1.0), axis=1, keepdims=True))
    bias_ref[...] = jnp.where(ge_c, 0.0, NEG)

    @pl.when(jnp.max(cnt) > n_sel)
    def _():
        sel = _tie_select(key, thr, n_sel, blk)
        bias_ref[...] = jnp.where(jnp.logical_and(sel, causal), 0.0, NEG)


def _attn_prompt_kernel(q_ref, k_ref, v_ref, qi_ref, ki_ref, wi_ref, o_ref, bias_ref, *, n_sel, q0):
    tq = q_ref.shape[1]
    width = k_ref.shape[1]
    ki = ki_ref[0]
    wi = wi_ref[0]
    score = jnp.zeros((tq, width), F32)
    for h in range(IDX_HEADS):
        lg = _dot_nt(qi_ref[0, :, h * 256:(h + 1) * 256], ki)
        score = score + wi[:, h:h + 1] * jnp.maximum(lg, 0.0)
    col = lax.broadcasted_iota(I32, (tq, width), 1)
    row = q0 + lax.broadcasted_iota(I32, (tq, width), 0)
    causal = col <= row
    _selection_bias(jnp.where(causal, score, NEG), causal, n_sel, RANK_BLK, bias_ref)
    bias = bias_ref[...]
    for h in range(N_HEADS_A):
        sl = slice(h * HEAD_A, (h + 1) * HEAD_A)
        s = _dot_nt(q_ref[0, :, sl], k_ref[0, :, sl]) + bias
        m = jnp.max(s, axis=1, keepdims=True)
        p = jnp.exp(s - m)
        l = jnp.sum(p, axis=1, keepdims=True)
        o = _dot(p.astype(BF16), v_ref[0, :, sl]) / l
        o_ref[0, :, sl] = o.astype(o_ref.dtype)


def _attn_prompt(q, k, v, qcat, kcat, wi):
    b, s_len, _ = q.shape
    tq = min(TQ, s_len)
    n_sel = min(TOPK, s_len // 4)
    outs = []
    for i in range(s_len // tq):
        width = (i + 1) * tq
        blk_q = lambda w, i=i: pl.BlockSpec((1, tq, w), lambda bi: (bi, i, 0))
        blk_k = lambda w, width=width: pl.BlockSpec((1, width, w), lambda bi: (bi, 0, 0))
        outs.append(pl.pallas_call(
            functools.partial(_attn_prompt_kernel, n_sel=n_sel, q0=i * tq),
            out_shape=jax.ShapeDtypeStruct((b, tq, C_A), BF16),
            grid=(b,),
            in_specs=[blk_q(C_A), blk_k(C_A), blk_k(C_A), blk_q(IDX_HEADS * 256), blk_k(256), blk_q(IDX_HEADS)],
            out_specs=pl.BlockSpec((1, tq, C_A), lambda bi: (bi, 0, 0)),
            scratch_shapes=[pltpu.VMEM((tq, width), F32)],
            compiler_params=_cparams(("parallel",)),
            name="attn_prompt_%d" % i,
        )(q, k, v, qcat, kcat, wi))
    return jnp.concatenate(outs, axis=1)


SEL_PAGES = 16
ATT_PAGES = 8


def _sel_sample_kernel(pt_ref, qh_ref, ql_ref, w_ref, knew_ref, *rest, n_sel, past, t_new):
    pages = rest[:SEL_PAGES]
    mask_ref = rest[SEL_PAGES]
    sc_ref, bias_ref = rest[SEL_PAGES + 1:]
    step = pl.program_id(1)
    qh, ql, w = qh_ref[0], ql_ref[0], w_ref[0]
    nq = qh.shape[0] // IDX_HEADS

    def scores_of(kf):
        kh, kl = _split_bf16(kf)
        lg = _dot_nt(qh, kh) + _dot_nt(ql, kh) + _dot_nt(qh, kl)
        x = w * jnp.maximum(lg, 0.0)
        return jnp.sum(x.reshape(nq, IDX_HEADS, kf.shape[0]), axis=1)

    span = SEL_PAGES * PAGE
    off = pl.multiple_of(step * span, span)
    sc_ref[0:nq, pl.ds(off, span)] = scores_of(jnp.concatenate([pg[0] for pg in pages], axis=0))

    @pl.when(step == pl.num_programs(1) - 1)
    def _():
        sc_ref[0:nq, past:past + PAGE] = scores_of(knew_ref[0])
        s = sc_ref[0:nq, :]
        col = lax.broadcasted_iota(I32, s.shape, 1)
        qpos = past + lax.broadcasted_iota(I32, s.shape, 0)
        causal = col <= qpos
        s = jnp.where(causal, s, NEG)
        s = jnp.where(col < past + t_new, s, -jnp.inf)
        bias = bias_ref.at[0:nq, :]
        _selection_bias(s, causal, n_sel, LANES, bias)
        mask_ref[0] = jnp.zeros(mask_ref.shape[1:], F32)
        mask_ref[0, 0:nq, :] = jnp.where(bias[...] > -1.0, 1.0, 0.0)


def _sel_sample(page_table, qh, ql, w, knew, cache_ki, t_new):
    b, n_pages = page_table.shape
    past = n_pages * PAGE
    n_sel = min(TOPK, (past + t_new) // 4)
    rows = qh.shape[1]
    width = past + PAGE
    steps = n_pages // SEL_PAGES
    page_spec = lambda u: pl.BlockSpec((1, PAGE, IDX_DIM), lambda bi, s, pt: (pt[bi, s * SEL_PAGES + u], 0, 0))
    per_b = lambda shp: pl.BlockSpec((1,) + shp, lambda bi, s, pt: (bi, 0, 0))
    return pl.pallas_call(
        functools.partial(_sel_sample_kernel, n_sel=n_sel, past=past, t_new=t_new),
        out_shape=jax.ShapeDtypeStruct((b, MASK_ROWS, width), F32),
        grid_spec=pltpu.PrefetchScalarGridSpec(
            num_scalar_prefetch=1,
            grid=(b, steps),
            in_specs=[per_b((rows, IDX_DIM)), per_b((rows, IDX_DIM)), per_b((rows, 1)), per_b((PAGE, IDX_DIM))]
                     + [page_spec(u) for u in range(SEL_PAGES)],
            out_specs=per_b((MASK_ROWS, width)),
            scratch_shapes=[pltpu.VMEM((8, width), F32), pltpu.VMEM((8, width), F32)],
        ),
        compiler_params=_cparams(("parallel", "arbitrary")),
        name="sel_sample",
    )(page_table, qh, ql, w, knew, *([cache_ki] * SEL_PAGES))


def _attn_sample_kernel(pt_ref, q_ref, knew_ref, vnew_ref, mnew_ref, mask_ref, *rest, nq):
    kpages = rest[:ATT_PAGES]
    vpages = rest[ATT_PAGES:2 * ATT_PAGES]
    o_ref = rest[2 * ATT_PAGES]
    m_ref, l_ref, acc_ref = rest[2 * ATT_PAGES + 1:]
    step = pl.program_id(1)
    q = q_ref[0]
    rows = q.shape[0]
    span = PAGE * N_HEADS_A
    tok = lax.broadcasted_iota(I32, (PAGE, span), 0)
    ln = lax.broadcasted_iota(I32, (PAGE, span), 1)
    expand = jnp.where(lax.shift_right_logical(ln, 3) == tok, 1.0, 0.0).astype(BF16)

    def valid_of(mk, width):
        e = _dot(mk.astype(BF16), expand[:, :width])
        sel = jnp.broadcast_to(e[:nq][:, None, :], (nq, N_HEADS_A, width)).reshape(rows, width) > 0.5
        lane_h = lax.broadcasted_iota(I32, (rows, width), 1) & (N_HEADS_A - 1)
        row_h = lax.broadcasted_iota(I32, (rows, width), 0) & (N_HEADS_A - 1)
        return jnp.logical_and(sel, lane_h == row_h)

    def update(kb, vb, valid, first):
        s = jnp.where(valid, _dot_nt(q, kb), NEG)
        m_blk = jnp.max(s, axis=1, keepdims=True)
        if first:
            m_new = m_blk
            p = jnp.where(valid, jnp.exp(s - m_new), 0.0)
            l_ref[...] = jnp.sum(p, axis=1, keepdims=True)
            acc_ref[...] = _dot(p.astype(BF16), vb)
        else:
            m_old = m_ref[...]
            m_new = jnp.maximum(m_old, m_blk)
            alpha = jnp.exp(m_old - m_new)
            p = jnp.where(valid, jnp.exp(s - m_new), 0.0)
            l_ref[...] = alpha * l_ref[...] + jnp.sum(p, axis=1, keepdims=True)
            acc_ref[...] = alpha * acc_ref[...] + _dot(p.astype(BF16), vb)
        m_ref[...] = m_new

    @pl.when(step == 0)
    def _():
        update(knew_ref[0], vnew_ref[0], valid_of(mnew_ref[0], PAGE), True)

    @pl.when(step > 0)
    def _():
        kb = jnp.concatenate([pg[0, 0].reshape(span, HEAD_A).astype(BF16) for pg in kpages], axis=0)
        vb = jnp.concatenate([pg[0, 0].reshape(span, HEAD_A).astype(BF16) for pg in vpages], axis=0)
        valid = jnp.concatenate([valid_of(mask_ref[0, :, u * PAGE:(u + 1) * PAGE], span)
                                 for u in range(ATT_PAGES)], axis=1)
        update(kb, vb, valid, False)

    @pl.when(step == pl.num_programs(1) - 1)
    def _():
        o_ref[0] = acc_ref[...] / l_ref[...]


def _attn_sample(page_table, layer, q, knew, vnew, mask, cache_k, cache_v):
    b, n_pages = page_table.shape
    rows = q.shape[1]
    nq = rows // N_HEADS_A
    past = n_pages * PAGE
    steps = n_pages // ATT_PAGES + 1
    pg = lambda s, u: jnp.maximum(s - 1, 0) * ATT_PAGES + u
    page_spec = lambda u: pl.BlockSpec((1, 1, PAGE, N_HEADS_A, HEAD_A),
                                       lambda bi, s, pt: (layer, pt[bi, pg(s, u)], 0, 0, 0))
    per_b = lambda shp: pl.BlockSpec((1,) + shp, lambda bi, s, pt: (bi, 0, 0))
    return pl.pallas_call(
        functools.partial(_attn_sample_kernel, nq=nq),
        out_shape=jax.ShapeDtypeStruct((b, rows, HEAD_A), F32),
        grid_spec=pltpu.PrefetchScalarGridSpec(
            num_scalar_prefetch=1,
            grid=(b, steps),
            in_specs=[per_b((rows, HEAD_A)), per_b((PAGE, HEAD_A)), per_b((PAGE, HEAD_A)),
                      pl.BlockSpec((1, MASK_ROWS, PAGE), lambda bi, s, pt: (bi, 0, past // PAGE)),
                      pl.BlockSpec((1, MASK_ROWS, ATT_PAGES * PAGE),
                                   lambda bi, s, pt: (bi, 0, jnp.maximum(s - 1, 0)))]
                     + [page_spec(u) for u in range(ATT_PAGES)] * 2,
            out_specs=per_b((rows, HEAD_A)),
            scratch_shapes=[pltpu.VMEM((rows, 1), F32), pltpu.VMEM((rows, 1), F32),
                            pltpu.VMEM((rows, HEAD_A), F32)],
        ),
        compiler_params=_cparams(("parallel", "arbitrary")),
        name="attn_sample",
    )(page_table, q, knew, vnew, mask, mask, *([cache_k] * ATT_PAGES), *([cache_v] * ATT_PAGES))


def _seg_sum(x, jmat):
    r = x.shape[0]
    xs = jnp.concatenate([x[:, p * LANES:(p + 1) * LANES] for p in range(N_PAIRS)], axis=0)
    hi, lo = _split_bf16(xs)
    ys = _dot(hi, jmat) + _dot(lo, jmat)
    return jnp.concatenate([ys[p * r:(p + 1) * r] for p in range(N_PAIRS)], axis=1)


def _rwkv_kernel(pb_ref, first_ref, s0_ref, mu_ref, w0_ref, a0_ref, kk_ref, ka_ref, rk_ref, lw_ref, lb_ref,
                 w2h, w2l, a2h, a2l, g2h, g2l, ob_ref, st_ref, carry_ref, state_ref, *, t_valid):
    nb, c = pb_ref.shape[0], pb_ref.shape[1]
    rows = nb * c
    n = pl.program_id(1)
    half = HEAD_B

    @pl.when(n == 0)
    def _():
        carry_ref[...] = first_ref[:, 0, :]
        z = jnp.zeros((half, half), F32)
        for b in range(nb):
            for p in range(N_PAIRS):
                top = jnp.concatenate([s0_ref[b, 2 * p], z], axis=1)
                bot = jnp.concatenate([z, s0_ref[b, 2 * p + 1]], axis=1)
                state_ref[b * N_PAIRS + p] = jnp.concatenate([top, bot], axis=0)

    pb = pb_ref[...].reshape(rows, PB_W)
    rowi = lax.broadcasted_iota(I32, (rows, 1), 0)
    step_i = rowi & (c - 1)
    carry = jnp.broadcast_to(carry_ref[...][:, None, :], (nb, c, PB_W)).reshape(rows, PB_W)
    prev = jnp.where(step_i == 0, carry, pltpu.roll(pb, 1, axis=0))
    carry_ref[...] = pb_ref[:, c - 1, :]
    xm = pb + (prev - pb) * mu_ref[...]

    r = xm[:, 0:C_B]
    k = xm[:, C_B:2 * C_B]
    v = xm[:, 2 * C_B:3 * C_B]
    xw = xm[:, 3 * C_B:3 * C_B + 128]
    xa = xm[:, 3 * C_B + 128:3 * C_B + 256]
    xg = xm[:, 3 * C_B + 256:3 * C_B + 512]
    w = -jax.nn.softplus(-(w0_ref[...] + _dot3w(jnp.tanh(xw), w2h, w2l))) - 0.5
    ld = -jnp.exp(w)
    a = jax.nn.sigmoid(a0_ref[...] + _dot3w(xa, a2h, a2l))
    g = _dot3w(jax.nn.sigmoid(xg), g2h, g2l)

    li = lax.broadcasted_iota(I32, (LANES, LANES), 0)
    lj = lax.broadcasted_iota(I32, (LANES, LANES), 1)
    same_head = (li // half) == (lj // half)
    jmat = jnp.where(same_head, 1.0, 0.0).astype(BF16)

    kkr = k * kk_ref[...]
    kk = kkr / jnp.maximum(jnp.sqrt(_seg_sum(kkr * kkr, jmat)), 1e-12)
    k2 = k * (1.0 + (a - 1.0) * ka_ref[...])
    bonus = _seg_sum(r * k2 * rk_ref[...], jmat)
    avec = -kk
    bvec = kk * a
    if t_valid < c:
        live = step_i < t_valid
        ld = jnp.where(live, ld, 0.0)
        avec = jnp.where(live, avec, 0.0)
        bvec = jnp.where(live, bvec, 0.0)
        k2s = jnp.where(live, k2, 0.0)
        vs = jnp.where(live, v, 0.0)
    else:
        k2s, vs = k2, v

    ti = lax.broadcasted_iota(I32, (rows, rows), 0)
    tj = lax.broadcasted_iota(I32, (rows, rows), 1)
    tri = jnp.where(jnp.logical_and(tj <= ti, (ti // c) == (tj // c)), 1.0, 0.0).astype(BF16)
    l1 = ld.astype(BF16)
    rem = ld - l1.astype(F32)
    l2 = rem.astype(BF16)
    l3 = (rem - l2.astype(F32)).astype(BF16)
    logc = _dot(tri, l1) + _dot(tri, l2) + _dot(tri, l3)
    logc_end = jnp.broadcast_to(logc.reshape(nb, c, C_B)[:, c - 1:c, :], (nb, c, C_B)).reshape(rows, C_B)
    e_pos = jnp.exp(logc)
    e_neg = jnp.exp(-logc)
    e_end = jnp.exp(logc_end - logc)

    def grp(x):
        return jnp.stack([x[b * c:(b + 1) * c, p * LANES:(p + 1) * LANES]
                          for b in range(nb) for p in range(N_PAIRS)], axis=0)

    rt = grp(r * e_pos)
    at = grp(avec * jnp.exp(logc - ld))
    kt = grp(k2s * e_neg)
    bt = grp(bvec * e_neg)
    kc = grp(k2s * e_end)
    bc = grp(bvec * e_end)
    vg = grp(vs)
    dec_end = grp(jnp.exp(logc_end))[:, 0:1, :]

    lane = lax.broadcasted_iota(I32, (1, 1, LANES), 2)
    m0 = lane < half
    col4 = lax.broadcasted_iota(I32, (1, c, 4 * c), 2) & (c - 1)
    row4 = lax.broadcasted_iota(I32, (1, c, 4 * c), 1)
    eye2 = jnp.where(lax.broadcasted_iota(I32, (1, 2 * c, 2 * c), 1)
                     == lax.broadcasted_iota(I32, (1, 2 * c, 2 * c), 2), 1.0, 0.0)
    lane2c = lax.broadcasted_iota(I32, (1, 1, 2 * c), 2) < c

    def two(x):
        return jnp.concatenate([jnp.where(m0, x, 0.0), jnp.where(m0, 0.0, x)], axis=1)

    s_old = state_ref[...]
    lhs1 = jnp.concatenate([rt, at], axis=1).astype(BF16)
    rhs1 = jnp.concatenate([two(kt), two(bt)], axis=1).astype(BF16)
    amat = _bmm_nt(lhs1, rhs1)
    a_r = jnp.where(col4 <= row4, amat[:, :c], 0.0)
    a_a = jnp.where(col4 < row4, amat[:, c:], 0.0)
    x0 = _bmm_nt(lhs1, s_old.astype(BF16))
    vm = two(vg).astype(BF16)
    x = x0[:, c:] + _bmm(a_a[:, :, :2 * c].astype(BF16), vm)
    lab = a_a[:, :, 2 * c:]
    lbd = jnp.concatenate([jnp.where(lane2c, lab, 0.0), jnp.where(lane2c, 0.0, lab)], axis=1)
    pm = eye2 + lbd
    lk = lbd
    for _ in range(int(np.log2(c)) - 1):
        lkb = lk.astype(BF16)
        lk = _bmm(lkb, lkb)
        pm = pm + _bmm(pm.astype(BF16), lk.astype(BF16))
    us = _bmm(pm.astype(BF16), two(x).astype(BF16))
    u = us[:, :c] + us[:, c:]
    y = x0[:, :c] + _bmm(a_r.astype(BF16), jnp.concatenate([vm, two(u).astype(BF16)], axis=1))
    vals = jnp.concatenate([vg, u], axis=1)
    keys = jnp.concatenate([kc, bc], axis=1).astype(BF16)
    add = _bmm(jnp.swapaxes(vals, 1, 2).astype(BF16), keys)
    state_ref[...] = s_old * dec_end + jnp.where(same_head[None], add, 0.0)

    y = jnp.concatenate([jnp.concatenate([y[b * N_PAIRS + p] for p in range(N_PAIRS)], axis=1)
                         for b in range(nb)], axis=0)
    mean = _seg_sum(y, jmat) * (1.0 / half)
    d = y - mean
    var = _seg_sum(d * d, jmat) * (1.0 / half)
    yn = d * lax.rsqrt(var + LNX_EPS) * lw_ref[...] + lb_ref[...]
    ob_ref[...] = ((yn + bonus * v) * g).astype(ob_ref.dtype).reshape(nb, c, C_B)

    @pl.when(n == pl.num_programs(1) - 1)
    def _():
        for b in range(nb):
            for p in range(N_PAIRS):
                s_new = state_ref[b * N_PAIRS + p]
                st_ref[b, 2 * p] = s_new[:half, :half]
                st_ref[b, 2 * p + 1] = s_new[half:, half:]


def _rwkv(pb, first, s0, vecs, loras, t_valid):
    b, t, _ = pb.shape
    nb = int(np.gcd(b, RWKV_NB))
    vec = lambda a: pl.BlockSpec(a.shape, lambda bi, n: (0, 0))
    st_spec = pl.BlockSpec((nb,) + s0.shape[1:], lambda bi, n: (bi, 0, 0, 0))
    return pl.pallas_call(
        functools.partial(_rwkv_kernel, t_valid=t_valid),
        out_shape=(jax.ShapeDtypeStruct((b, t, C_B), BF16), jax.ShapeDtypeStruct(s0.shape, F32)),
        grid=(b // nb, t // CHUNK),
        in_specs=[pl.BlockSpec((nb, CHUNK, PB_W), lambda bi, n: (bi, n, 0)),
                  pl.BlockSpec((nb, 1, PB_W), lambda bi, n: (bi, 0, 0)), st_spec]
                 + [vec(a) for a in vecs] + [vec(a) for a in loras],
        out_specs=(pl.BlockSpec((nb, CHUNK, C_B), lambda bi, n: (bi, n, 0)), st_spec),
        scratch_shapes=[pltpu.VMEM((nb, PB_W), F32), pltpu.VMEM((nb * N_PAIRS, LANES, LANES), F32)],
        compiler_params=_cparams(("parallel", "arbitrary")),
        name="rwkv7",
    )(pb, first, s0, *vecs, *loras)


def _merge_kernel(oa_ref, ob_ref, wpa_ref, wpb_ref, ga_ref, gb_ref, o_ref):
    za = _dot(oa_ref[...], wpa_ref[...])
    zb = _dot(ob_ref[...], wpb_ref[...])
    o_ref[...] = (ga_ref[...].astype(F32) * za + gb_ref[...].astype(F32) * zb).astype(o_ref.dtype)


def _merge(oa, ob, wpa, wpb, gates, tn=1024):
    m = oa.shape[0]
    tm = _row_tile(m)
    nb = D_MODEL // tn
    return pl.pallas_call(
        _merge_kernel,
        out_shape=jax.ShapeDtypeStruct((m, D_MODEL), BF16),
        grid=(m // tm, nb),
        in_specs=[pl.BlockSpec((tm, C_A), lambda i, j: (i, 0)), pl.BlockSpec((tm, C_B), lambda i, j: (i, 0)),
                  pl.BlockSpec((C_A, tn), lambda i, j: (0, j)), pl.BlockSpec((C_B, tn), lambda i, j: (0, j)),
                  pl.BlockSpec((tm, tn), lambda i, j: (i, j)), pl.BlockSpec((tm, tn), lambda i, j: (i, j + nb))],
        out_specs=pl.BlockSpec((tm, tn), lambda i, j: (i, j)),
        compiler_params=_cparams(("parallel", "parallel")),
        name="merge",
    )(oa, ob, wpa, wpb, gates, gates)


def _oproj_kernel(x_ref, mix_ref, wo_ref, g_ref, x1_ref, h2_ref):
    x1 = x_ref[...] + _dot(mix_ref[...], wo_ref[...])
    x1_ref[...] = x1
    h2 = x1 * lax.rsqrt(jnp.mean(x1 * x1, axis=-1, keepdims=True) + NORM_EPS) * g_ref[...]
    h2_ref[...] = h2.astype(BF16)


def _oproj(x, mixed, wo, g):
    m, d = x.shape
    tm = _row_tile(m)
    row = pl.BlockSpec((tm, d), lambda i: (i, 0))
    return pl.pallas_call(
        _oproj_kernel,
        out_shape=(jax.ShapeDtypeStruct((m, d), F32), jax.ShapeDtypeStruct((m, d), BF16)),
        grid=(m // tm,),
        in_specs=[row, row, pl.BlockSpec((d, d), lambda i: (0, 0)), pl.BlockSpec((1, d), lambda i: (0, 0))],
        out_specs=(row, row),
        compiler_params=_cparams(("parallel",)),
        name="oproj_norm",
    )(x, mixed, wo, g.reshape(1, d))


def _ffn_up_kernel(h_ref, wg_ref, wu_ref, o_ref):
    h = h_ref[...]
    gate = _dot(h, wg_ref[...])
    up = _dot(h, wu_ref[...])
    o_ref[...] = (gate * jax.nn.sigmoid(gate) * up).astype(o_ref.dtype)


def _ffn_up(h2, wg, wu, tn=512):
    m, d = h2.shape
    n = wg.shape[1]
    tm = _row_tile(m)
    return pl.pallas_call(
        _ffn_up_kernel,
        out_shape=jax.ShapeDtypeStruct((m, n), BF16),
        grid=(m // tm, n // tn),
        in_specs=[pl.BlockSpec((tm, d), lambda i, j: (i, 0)), pl.BlockSpec((d, tn), lambda i, j: (0, j)),
                  pl.BlockSpec((d, tn), lambda i, j: (0, j))],
        out_specs=pl.BlockSpec((tm, tn), lambda i, j: (i, j)),
        compiler_params=_cparams(("parallel", "parallel")),
        name="ffn_up",
    )(h2, wg, wu)


def _ffn_down_kernel(x_ref, a_ref, wd_ref, o_ref):
    o_ref[...] = x_ref[...] + _dot(a_ref[...], wd_ref[...])


def _ffn_down(x1, act, wd, tn=1024):
    m, d = x1.shape
    kdim = act.shape[1]
    tm = _row_tile(m)
    return pl.pallas_call(
        _ffn_down_kernel,
        out_shape=jax.ShapeDtypeStruct((m, d), F32),
        grid=(m // tm, d // tn),
        in_specs=[pl.BlockSpec((tm, tn), lambda i, j: (i, j)), pl.BlockSpec((tm, kdim), lambda i, j: (i, 0)),
                  pl.BlockSpec((kdim, tn), lambda i, j: (0, j))],
        out_specs=pl.BlockSpec((tm, tn), lambda i, j: (i, j)),
        compiler_params=_cparams(("parallel", "parallel")),
        name="ffn_down",
    )(x1, act, wd)


def _pad_cols(w, width):
    return jnp.pad(w, ((0, 0), (0, width - w.shape[1])))


def _pad_rows(w, height):
    return jnp.pad(w, ((0, height - w.shape[0]), (0, 0)))


def _prep_weights(w_in, mu_b, w2, a2, g2):
    o = 0
    cols = {}
    for name, wdt in (("q", C_A), ("k", C_A), ("v", C_A), ("qi", IDX_HEADS * IDX_DIM), ("ki", IDX_DIM),
                      ("wi", IDX_HEADS), ("g", 2 * D_MODEL), ("rkv", 3 * C_B), ("xw", D_DECAY), ("xa", D_AAA),
                      ("xg", D_GATE)):
        cols[name] = (o, o + wdt)
        o += wdt
    cut = lambda name: w_in[:, cols[name][0]:cols[name][1]]
    w_idx = _pad_cols(jnp.concatenate([cut("qi"), cut("ki"), cut("wi")], axis=1), IDX_W)
    idx_hi, idx_lo = _split_bf16(w_idx)
    w_b = jnp.concatenate([cut("rkv"), _pad_cols(cut("xw"), 128), _pad_cols(cut("xa"), 128),
                           _pad_cols(cut("xg"), 256)], axis=1)
    nb0 = cols["rkv"][0]
    mu = lambda name: mu_b[cols[name][0] - nb0:cols[name][1] - nb0][None, :]
    mu_p = jnp.concatenate([mu("rkv"), _pad_cols(mu("xw"), 128), _pad_cols(mu("xa"), 128),
                            _pad_cols(mu("xg"), 256)], axis=1)
    loras = []
    for wl, hgt in ((w2, 128), (a2, 128), (g2, 256)):
        loras += list(_split_bf16(_pad_rows(wl, hgt)))
    return dict(q=cut("q").astype(BF16), k=cut("k").astype(BF16), v=cut("v").astype(BF16),
                idx_hi=idx_hi, idx_lo=idx_lo, g=cut("g").astype(BF16), b=w_b.astype(BF16), mu=mu_p,
                loras=loras)


def _layer(x, pos, wts, p, *, seq, attn_fn, first_pb, s0, t_valid):
    m = x.shape[0]
    nb = m // seq
    hh, hl = _norm_split(x, p["g_mix"])
    tabs_a = _rope_tables(pos, HEAD_A)
    tabs_i = _rope_tables(pos, IDX_DIM)
    (q_b,) = _proj(hh, wts["q"], tabs_a, rope=True, want_f32=False, want_bf16=True, scale=HEAD_A ** -0.5)
    k_f, k_b = _proj(hh, wts["k"], tabs_a, rope=True, want_f32=True, want_bf16=True)
    v_f, v_b = _proj(hh, wts["v"], tabs_a, rope=False, want_f32=True, want_bf16=True)
    qcat, kcat, ki_f, wi = _idx_proj(hh, hl, wts["idx_hi"], wts["idx_lo"], tabs_i)
    gates = _mm(hh, wts["g"], tn=1024, out_dtype=BF16, act="sigmoid")
    pb = _mm(hh, wts["b"], tn=PB_W // 4, out_dtype=F32)

    o_a = attn_fn(q_b, k_b, v_b, qcat, kcat, ki_f, wi)

    t_pad = -(-seq // CHUNK) * CHUNK
    pb3 = pb.reshape(nb, seq, PB_W)
    if t_pad != seq:
        pb3 = jnp.pad(pb3, ((0, 0), (0, t_pad - seq), (0, 0)))
    vecs = [wts["mu"]] + [p[n].reshape(1, C_B) for n in ("w0", "a0", "k_k", "k_a", "r_k", "lnx_w", "lnx_b")]
    o_b, wkv = _rwkv(pb3, first_pb, s0, vecs, wts["loras"], t_valid)
    o_b = o_b[:, :seq].reshape(m, C_B)

    mixed = _merge(o_a, o_b, p["w_pa"], p["w_pb"], gates)
    x1, h2 = _oproj(x, mixed, p["w_o"], p["g_ffn"])
    act = _ffn_up(h2, p["w_gate"], p["w_up"])
    x2 = _ffn_down(x1, act, p["w_down"])
    shift = _norm_f32(x.reshape(nb, seq, D_MODEL)[:, -1], p["g_mix"])
    return x2, k_f, v_f, ki_f, wkv, shift


def kernel(x_prompt, x_sample, cache_k, cache_v, cache_kidx, state_wkv, state_shift, page_table, g_mix, w_in, mu_b, w0, w2, a0, a2, g2, k_k, k_a, r_k, lnx_w, lnx_b, w_pa, w_pb, w_o, g_ffn, w_gate, w_up, w_down, g_final):
    bp, sp, d = x_prompt.shape
    bs, ts, _ = x_sample.shape
    depth = w_in.shape[0]
    n_pages = page_table.shape[1]
    past = n_pages * PAGE
    xp = x_prompt.reshape(bp * sp, d)
    xs = x_sample.reshape(bs * ts, d)
    outs_p, outs_s = [], []
    for l in range(depth):
        wts = _prep_weights(w_in[l], mu_b[l], w2[l], a2[l], g2[l])
        p = dict(g_mix=g_mix[l], w0=w0[l], a0=a0[l], k_k=k_k[l], k_a=k_a[l], r_k=r_k[l], lnx_w=lnx_w[l],
                 lnx_b=lnx_b[l], w_pa=w_pa[l].astype(BF16), w_pb=w_pb[l].astype(BF16), w_o=w_o[l].astype(BF16),
                 g_ffn=g_ffn[l], w_gate=w_gate[l].astype(BF16), w_up=w_up[l].astype(BF16),
                 w_down=w_down[l].astype(BF16))

        def attn_p(q_b, k_b, v_b, qcat, kcat, ki_f, wi):
            r3 = lambda a: a.reshape(bp, sp, a.shape[-1])
            return _attn_prompt(r3(q_b), r3(k_b), r3(v_b), r3(qcat), r3(kcat), r3(wi)).reshape(bp * sp, C_A)

        def attn_s(q_b, k_b, v_b, qcat, kcat, ki_f, wi):
            qc = qcat.reshape(bs, ts * IDX_HEADS, 256)
            qh, ql = qc[..., :IDX_DIM], qc[..., IDX_DIM:2 * IDX_DIM]
            wrow = wi.reshape(bs, ts * IDX_HEADS, 1)
            ki_new = jnp.pad(ki_f.reshape(bs, ts, IDX_DIM), ((0, 0), (0, PAGE - ts), (0, 0)))
            mask = _sel_sample(page_table, qh, ql, wrow, ki_new,
                               cache_kidx[l].reshape(-1, PAGE, IDX_DIM), ts)
            as_rows = lambda a: a.reshape(bs, ts * N_HEADS_A, HEAD_A)
            pad_rows = lambda a: jnp.pad(as_rows(a), ((0, 0), (0, PAGE - ts * N_HEADS_A), (0, 0)))
            o = _attn_sample(page_table, l, as_rows(q_b), pad_rows(k_b), pad_rows(v_b), mask, cache_k, cache_v)
            return o.reshape(bs * ts, C_A).astype(BF16)

        zero_first = jnp.zeros((bp, 1, PB_W), F32)
        zero_state = jnp.zeros((bp, N_HEADS_B, HEAD_B, HEAD_B), F32)
        xp, kp, vp, kip, wkvp, shp = _layer(xp, jnp.arange(sp), wts, p, seq=sp, attn_fn=attn_p,
                                            first_pb=zero_first, s0=zero_state, t_valid=CHUNK)
        first_s = _mm(state_shift[l].astype(BF16), wts["b"], tn=PB_W // 4, out_dtype=F32).reshape(bs, 1, PB_W)
        pos_s = jnp.tile(past + jnp.arange(ts), bs)
        xs, ksm, vsm, kis, wkvs, shs = _layer(xs, pos_s, wts, p, seq=ts, attn_fn=attn_s,
                                              first_pb=first_s, s0=state_wkv[l], t_valid=ts)
        outs_p.append((kp.reshape(bp, sp, N_HEADS_A, HEAD_A), vp.reshape(bp, sp, N_HEADS_A, HEAD_A),
                       kip.reshape(bp, sp, IDX_DIM), wkvp, shp))
        outs_s.append((ksm.reshape(bs, ts, N_HEADS_A, HEAD_A), vsm.reshape(bs, ts, N_HEADS_A, HEAD_A),
                       kis.reshape(bs, ts, IDX_DIM), wkvs, shs))
    y_prompt = _norm_f32(xp, g_final).reshape(bp, sp, d)
    y_sample = _norm_f32(xs, g_final).reshape(bs, ts, d)
    stack = lambda outs, i: jnp.stack([o[i] for o in outs])
    return ((y_prompt, y_sample) + tuple(stack(outs_p, i) for i in range(5))
            + tuple(stack(outs_s, i) for i in range(5)))
```

```python
import functools

import numpy as np
import jax
import jax.numpy as jnp
from jax import lax
from jax.experimental import pallas as pl
from jax.experimental.pallas import tpu as pltpu

F32 = jnp.float32
BF16 = jnp.bfloat16
I32 = jnp.int32

D_MODEL = 2048
HEAD_A = 128
N_HEADS_A = 8
C_A = N_HEADS_A * HEAD_A
IDX_HEADS = 8
IDX_DIM = 64
TOPK = 256
HEAD_B = 64
N_HEADS_B = 16
C_B = N_HEADS_B * HEAD_B
D_DECAY = 64
D_AAA = 64
D_GATE = 160
LNX_EPS = 64e-5
ROPE_THETA = 500000.0
NORM_EPS = 1e-6
PAGE = 128
NEG = -1e30
INT_MIN = -(2 ** 31)

LANES = 128
VMEM_LIMIT = 56 * 1024 * 1024
CHUNK = 64
RWKV_NB = 4
N_PAIRS = N_HEADS_B // 2
TQ = 256
RANK_BLK = 256
MASK_ROWS = 16
PB_W = 3 * C_B + 128 + 128 + 256
IDX_W = 640


def _cparams(sem):
    return pltpu.CompilerParams(dimension_semantics=sem, vmem_limit_bytes=VMEM_LIMIT)


def _split_bf16(x):
    hi = x.astype(BF16)
    lo = (x - hi.astype(F32)).astype(BF16)
    return hi, lo


def _dot(a, b):
    return jnp.dot(a, b, preferred_element_type=F32)


def _dot_nt(a, b):
    return lax.dot_general(a, b, (((1,), (1,)), ((), ())), preferred_element_type=F32)


def _bmm(a, b):
    return lax.dot_general(a, b, (((2,), (1,)), ((0,), (0,))), preferred_element_type=F32)


def _bmm_nt(a, b):
    return lax.dot_general(a, b, (((2,), (2,)), ((0,), (0,))), preferred_element_type=F32)


def _dot3w(x, wh_ref, wl_ref):
    xh, xl = _split_bf16(x)
    wh = wh_ref[...]
    return _dot(xh, wh) + _dot(xl, wh) + _dot(xh, wl_ref[...])


def _norm_split_kernel(x_ref, g_ref, hi_ref, lo_ref):
    x = x_ref[...]
    y = x * lax.rsqrt(jnp.mean(x * x, axis=-1, keepdims=True) + NORM_EPS) * g_ref[...]
    hi = y.astype(BF16)
    hi_ref[...] = hi
    lo_ref[...] = (y - hi.astype(F32)).astype(BF16)


def _norm_f32_kernel(x_ref, g_ref, o_ref):
    x = x_ref[...]
    o_ref[...] = x * lax.rsqrt(jnp.mean(x * x, axis=-1, keepdims=True) + NORM_EPS) * g_ref[...]


def _row_tile(m, pref=512):
    for tm in (pref, 512):
        if m % tm == 0:
            return tm
    return m


def _norm_split(x, g):
    m, d = x.shape
    tm = _row_tile(m)
    return pl.pallas_call(
        _norm_split_kernel,
        out_shape=(jax.ShapeDtypeStruct((m, d), BF16), jax.ShapeDtypeStruct((m, d), BF16)),
        grid=(m // tm,),
        in_specs=[pl.BlockSpec((tm, d), lambda i: (i, 0)), pl.BlockSpec((1, d), lambda i: (0, 0))],
        out_specs=(pl.BlockSpec((tm, d), lambda i: (i, 0)), pl.BlockSpec((tm, d), lambda i: (i, 0))),
        compiler_params=_cparams(("parallel",)),
        name="norm_split",
    )(x, g.reshape(1, d))


def _norm_f32(x, g):
    m, d = x.shape
    tm = _row_tile(m)
    return pl.pallas_call(
        _norm_f32_kernel,
        out_shape=jax.ShapeDtypeStruct((m, d), F32),
        grid=(m // tm,),
        in_specs=[pl.BlockSpec((tm, d), lambda i: (i, 0)), pl.BlockSpec((1, d), lambda i: (0, 0))],
        out_specs=pl.BlockSpec((tm, d), lambda i: (i, 0)),
        compiler_params=_cparams(("parallel",)),
        name="norm_f32",
    )(x, g.reshape(1, d))


def _rope_tables(pos, head_dim):
    rot = head_dim // 4
    half = rot // 2
    inv = ROPE_THETA ** (-jnp.arange(half, dtype=F32) * 2.0 / rot)
    ang = pos.astype(F32)[:, None] * inv[None, :]
    cos, sin = jnp.cos(ang), jnp.sin(ang)
    t = pos.shape[0]
    c = jnp.ones((t, head_dim), F32).at[:, :half].set(cos).at[:, half:rot].set(cos)
    s1 = jnp.zeros((t, head_dim), F32).at[:, half:rot].set(sin)
    s2 = jnp.zeros((t, head_dim), F32).at[:, :half].set(-sin)
    rep = LANES // head_dim
    return tuple(jnp.tile(a, (1, rep)) for a in (c, s1, s2))


def _rope_lanes(x, c, s1, s2, half):
    return x * c + pltpu.roll(x, half, axis=1) * s1 + pltpu.roll(x, LANES - half, axis=1) * s2


def _proj_kernel(a_ref, w_ref, c_ref, s1_ref, s2_ref, *out_refs, rope, want_f32, want_bf16, scale):
    acc = _dot(a_ref[...], w_ref[...])
    if rope:
        c, s1, s2 = c_ref[...], s1_ref[...], s2_ref[...]
        parts = [_rope_lanes(acc[:, h * LANES:(h + 1) * LANES], c, s1, s2, HEAD_A // 8)
                 for h in range(acc.shape[1] // LANES)]
        acc = jnp.concatenate(parts, axis=1)
    if scale is not None:
        acc = acc * scale
    k = 0
    if want_f32:
        out_refs[k][...] = acc
        k += 1
    if want_bf16:
        out_refs[k][...] = acc.astype(BF16)


def _proj(a, w, tabs, *, rope, want_f32, want_bf16, scale=None):
    m, kdim = a.shape
    n = w.shape[1]
    tm = _row_tile(m)
    ntab = tabs[0].shape[0] // tm
    outs, ospecs = [], []
    for want, dt in ((want_f32, F32), (want_bf16, BF16)):
        if want:
            outs.append(jax.ShapeDtypeStruct((m, n), dt))
            ospecs.append(pl.BlockSpec((tm, n), lambda i: (i, 0)))
    tab_spec = pl.BlockSpec((tm, LANES), lambda i: (i % ntab, 0))
    res = pl.pallas_call(
        functools.partial(_proj_kernel, rope=rope, want_f32=want_f32, want_bf16=want_bf16, scale=scale),
        out_shape=tuple(outs),
        grid=(m // tm,),
        in_specs=[pl.BlockSpec((tm, kdim), lambda i: (i, 0)), pl.BlockSpec((kdim, n), lambda i: (0, 0)),
                  tab_spec, tab_spec, tab_spec],
        out_specs=tuple(ospecs),
        compiler_params=_cparams(("parallel",)),
        name="proj_rope" if rope else "proj",
    )(a, w, *tabs)
    return res


def _mm_kernel(a_ref, w_ref, o_ref, *, act):
    acc = _dot(a_ref[...], w_ref[...])
    if act == "sigmoid":
        acc = jax.nn.sigmoid(acc)
    o_ref[...] = acc.astype(o_ref.dtype)


def _mm(a, w, *, tn, out_dtype, act=None):
    m, kdim = a.shape
    n = w.shape[1]
    tm = _row_tile(m, 1024)
    return pl.pallas_call(
        functools.partial(_mm_kernel, act=act),
        out_shape=jax.ShapeDtypeStruct((m, n), out_dtype),
        grid=(m // tm, n // tn),
        in_specs=[pl.BlockSpec((tm, kdim), lambda i, j: (i, 0)), pl.BlockSpec((kdim, tn), lambda i, j: (0, j))],
        out_specs=pl.BlockSpec((tm, tn), lambda i, j: (i, j)),
        compiler_params=_cparams(("parallel", "parallel")),
        name="mm_" + (act or "plain"),
    )(a, w)


def _idx_proj_kernel(hh_ref, hl_ref, wh_ref, wl_ref, c_ref, s1_ref, s2_ref,
                     qcat_ref, kcat_ref, kf_ref, wi_ref):
    hh = hh_ref[...]
    wh = wh_ref[...]
    acc = _dot(hh, wh) + _dot(hl_ref[...], wh) + _dot(hh, wl_ref[...])
    c, s1, s2 = c_ref[...], s1_ref[...], s2_ref[...]
    half = IDX_DIM // 8
    zeros = jnp.zeros((acc.shape[0], IDX_DIM), F32)

    def hi_lo(x):
        hi = x.astype(BF16).astype(F32)
        return hi, (x - hi).astype(BF16).astype(F32)

    pieces = []
    for g in range(IDX_HEADS * IDX_DIM // LANES):
        blk = _rope_lanes(acc[:, g * LANES:(g + 1) * LANES], c, s1, s2, half)
        for u in range(LANES // IDX_DIM):
            hi, lo = hi_lo(blk[:, u * IDX_DIM:(u + 1) * IDX_DIM])
            pieces += [hi, lo, hi, zeros]
    qcat_ref[...] = jnp.concatenate(pieces, axis=1).astype(BF16)
    last = acc[:, IDX_HEADS * IDX_DIM:]
    ki = _rope_lanes(last, c, s1, s2, half)[:, :IDX_DIM]
    kf_ref[...] = ki
    hi, lo = hi_lo(ki)
    kcat_ref[...] = jnp.concatenate([hi, hi, lo, zeros], axis=1).astype(BF16)
    wi_ref[...] = last[:, IDX_DIM:IDX_DIM + IDX_HEADS] * (IDX_HEADS ** -0.5 * IDX_DIM ** -0.5)


def _idx_proj(hh, hl, wh, wl, tabs):
    m, d = hh.shape
    tm = _row_tile(m)
    ntab = tabs[0].shape[0] // tm
    row = lambda w: pl.BlockSpec((tm, w), lambda i: (i, 0))
    full = lambda a: pl.BlockSpec(a.shape, lambda i: (0, 0))
    tab_spec = pl.BlockSpec((tm, LANES), lambda i: (i % ntab, 0))
    return pl.pallas_call(
        _idx_proj_kernel,
        out_shape=(jax.ShapeDtypeStruct((m, IDX_HEADS * 256), BF16), jax.ShapeDtypeStruct((m, 256), BF16),
                   jax.ShapeDtypeStruct((m, IDX_DIM), F32), jax.ShapeDtypeStruct((m, IDX_HEADS), F32)),
        grid=(m // tm,),
        in_specs=[row(d), row(d), full(wh), full(wl), tab_spec, tab_spec, tab_spec],
        out_specs=(row(IDX_HEADS * 256), row(256), row(IDX_DIM), row(IDX_HEADS)),
        compiler_params=_cparams(("parallel",)),
        name="idx_proj",
    )(hh, hl, wh, wl, *tabs)


def _sort_key(s):
    bits = lax.bitcast_convert_type(s, I32)
    return bits ^ (lax.shift_right_arithmetic(bits, 31) & 0x7FFFFFFF)


def _kth_largest_key(key, n_sel):
    r = key.shape[0]

    def body(it, prefix):
        cand_u = prefix | lax.shift_left(jnp.int32(1), 31 - it)
        cand_s = cand_u ^ INT_MIN
        cnt = jnp.sum(jnp.where(key >= cand_s, 1.0, 0.0), axis=1, keepdims=True)
        return jnp.where(cnt >= n_sel, cand_u, prefix)

    return lax.fori_loop(0, 32, body, jnp.zeros((r, 1), I32)) ^ INT_MIN


def _tie_select(key, thr, n_sel, blk):
    r, l = key.shape
    gt = key > thr
    eq = key == thr
    need = n_sel - jnp.sum(jnp.where(gt, 1.0, 0.0), axis=1, keepdims=True)
    rr = lax.broadcasted_iota(I32, (blk, blk), 0)
    cc = lax.broadcasted_iota(I32, (blk, blk), 1)
    upper = jnp.where(rr < cc, 1.0, 0.0).astype(BF16)
    offs = jnp.zeros((r, 1), F32)
    parts = []
    for b in range(l // blk):
        e = jnp.where(eq[:, b * blk:(b + 1) * blk], 1.0, 0.0)
        rank = _dot(e.astype(BF16), upper) + offs
        parts.append(jnp.logical_and(eq[:, b * blk:(b + 1) * blk], rank < need))
        offs = offs + jnp.sum(e, axis=1, keepdims=True)
    return jnp.logical_or(gt, jnp.concatenate(parts, axis=1))


def _selection_bias(s, causal, n_sel, blk, bias_ref):
    key = _sort_key(s)
    thr = _kth_largest_key(key, n_sel)
    ge_c = jnp.logical_and(key >= thr, causal)
    cnt = (jnp.sum(jnp.where(ge_c, 1.0, 0.0), axis=1, keepdims=True)
           + jnp.sum(jnp.where(jnp.logical_or(key <= thr, causal), 0.0, 1.0), axis=1, keepdims=True))
    bias_ref[...] = jnp.where(ge_c, 0.0, NEG)

    @pl.when(jnp.max(cnt) > n_sel)
    def _():
        sel = _tie_select(key, thr, n_sel, blk)
        bias_ref[...] = jnp.where(jnp.logical_and(sel, causal), 0.0, NEG)


def _attn_prompt_kernel(q_ref, k_ref, v_ref, qi_ref, ki_ref, wi_ref, o_ref, bias_ref, *, n_sel, q0):
    tq = q_ref.shape[1]
    width = k_ref.shape[1]
    ki = ki_ref[0]
    wi = wi_ref[0]
    score = jnp.zeros((tq, width), F32)
    for h in range(IDX_HEADS):
        lg = _dot_nt(qi_ref[0, :, h * 256:(h + 1) * 256], ki)
        score = score + wi[:, h:h + 1] * jnp.maximum(lg, 0.0)
    col = lax.broadcasted_iota(I32, (tq, width), 1)
    row = q0 + lax.broadcasted_iota(I32, (tq, width), 0)
    causal = col <= row
    _selection_bias(jnp.where(causal, score, NEG), causal, n_sel, RANK_BLK, bias_ref)
    bias = bias_ref[...]
    for h in range(N_HEADS_A):
        sl = slice(h * HEAD_A, (h + 1) * HEAD_A)
        s = _dot_nt(q_ref[0, :, sl], k_ref[0, :, sl]) + bias
        m = jnp.max(s, axis=1, keepdims=True)
        p = jnp.exp(s - m)
        l = jnp.sum(p, axis=1, keepdims=True)
        o = _dot(p.astype(BF16), v_ref[0, :, sl]) / l
        o_ref[0, :, sl] = o.astype(o_ref.dtype)


def _attn_prompt(q, k, v, qcat, kcat, wi):
    b, s_len, _ = q.shape
    tq = min(TQ, s_len)
    n_sel = min(TOPK, s_len // 4)
    outs = []
    for i in range(s_len // tq):
        width = (i + 1) * tq
        blk_q = lambda w, i=i: pl.BlockSpec((1, tq, w), lambda bi: (bi, i, 0))
        blk_k = lambda w, width=width: pl.BlockSpec((1, width, w), lambda bi: (bi, 0, 0))
        outs.append(pl.pallas_call(
            functools.partial(_attn_prompt_kernel, n_sel=n_sel, q0=i * tq),
            out_shape=jax.ShapeDtypeStruct((b, tq, C_A), BF16),
            grid=(b,),
            in_specs=[blk_q(C_A), blk_k(C_A), blk_k(C_A), blk_q(IDX_HEADS * 256), blk_k(256), blk_q(IDX_HEADS)],
            out_specs=pl.BlockSpec((1, tq, C_A), lambda bi: (bi, 0, 0)),
            scratch_shapes=[pltpu.VMEM((tq, width), F32)],
            compiler_params=_cparams(("parallel",)),
            name="attn_prompt_%d" % i,
        )(q, k, v, qcat, kcat, wi))
    return jnp.concatenate(outs, axis=1)


SEL_PAGES = 16
ATT_PAGES = 8


def _score_sample_kernel(pt_ref, qh_ref, ql_ref, w_ref, knew_ref, *rest, past):
    pages = rest[:SEL_PAGES]
    sc_ref = rest[SEL_PAGES]
    step = pl.program_id(1)
    qh, ql, w = qh_ref[0], ql_ref[0], w_ref[0]
    nq = qh.shape[0] // IDX_HEADS

    def scores_of(kf):
        kh, kl = _split_bf16(kf)
        lg = _dot_nt(qh, kh) + _dot_nt(ql, kh) + _dot_nt(qh, kl)
        x = w * jnp.maximum(lg, 0.0)
        return jnp.sum(x.reshape(nq, IDX_HEADS, kf.shape[0]), axis=1)

    @pl.when(step == 0)
    def _():
        sc_ref[0] = jnp.zeros(sc_ref.shape[1:], F32)
        sc_ref[0, 0:nq, past:past + PAGE] = scores_of(knew_ref[0])

    span = SEL_PAGES * PAGE
    off = pl.multiple_of(step * span, span)
    sc_ref[0, 0:nq, pl.ds(off, span)] = scores_of(jnp.concatenate([pg[0, 0] for pg in pages], axis=0))


def _select_sample_kernel(s_ref, mask_ref, bias_ref, *, n_sel, past, t_new, nq):
    s = s_ref[...]
    col = lax.broadcasted_iota(I32, s.shape, 1)
    qpos = past + (lax.broadcasted_iota(I32, s.shape, 0) & (nq - 1))
    causal = col <= qpos
    s = jnp.where(causal, s, NEG)
    s = jnp.where(col < past + t_new, s, -jnp.inf)
    _selection_bias(s, causal, n_sel, LANES, bias_ref)
    mask_ref[...] = jnp.where(bias_ref[...] > -1.0, 1.0, 0.0)


def _sel_sample(page_table, layer, qh, ql, w, knew, cache_ki, t_new):
    b, n_pages = page_table.shape
    past = n_pages * PAGE
    n_sel = min(TOPK, (past + t_new) // 4)
    rows = qh.shape[1]
    nq = rows // IDX_HEADS
    assert nq & (nq - 1) == 0
    width = past + PAGE
    steps = n_pages // SEL_PAGES
    page_spec = lambda u: pl.BlockSpec((1, 1, PAGE, IDX_DIM),
                                       lambda bi, s, pt: (layer, pt[bi, s * SEL_PAGES + u], 0, 0))
    per_b = lambda shp: pl.BlockSpec((1,) + shp, lambda bi, s, pt: (bi, 0, 0))
    scores = pl.pallas_call(
        functools.partial(_score_sample_kernel, past=past),
        out_shape=jax.ShapeDtypeStruct((b, 8, width), F32),
        grid_spec=pltpu.PrefetchScalarGridSpec(
            num_scalar_prefetch=1,
            grid=(b, steps),
            in_specs=[per_b((rows, IDX_DIM)), per_b((rows, IDX_DIM)), per_b((rows, 1)), per_b((PAGE, IDX_DIM))]
                     + [page_spec(u) for u in range(SEL_PAGES)],
            out_specs=per_b((8, width)),
        ),
        compiler_params=_cparams(("parallel", "arbitrary")),
        name="score_sample",
    )(page_table, qh, ql, w, knew, *([cache_ki] * SEL_PAGES))
    flat = scores[:, :nq].reshape(b * nq, width)
    mask = pl.pallas_call(
        functools.partial(_select_sample_kernel, n_sel=n_sel, past=past, t_new=t_new, nq=nq),
        out_shape=jax.ShapeDtypeStruct((b * nq, width), F32),
        scratch_shapes=[pltpu.VMEM((b * nq, width), F32)],
        compiler_params=pltpu.CompilerParams(vmem_limit_bytes=VMEM_LIMIT),
        name="select_sample",
    )(flat)
    return jnp.pad(mask.reshape(b, nq, width), ((0, 0), (0, MASK_ROWS - nq), (0, 0)))


def _attn_sample_kernel(pt_ref, q_ref, knew_ref, vnew_ref, mnew_ref, mask_ref, *rest, nq):
    kpages = rest[:ATT_PAGES]
    vpages = rest[ATT_PAGES:2 * ATT_PAGES]
    o_ref = rest[2 * ATT_PAGES]
    m_ref, l_ref, acc_ref = rest[2 * ATT_PAGES + 1:]
    step = pl.program_id(1)
    q = q_ref[0]
    rows = q.shape[0]
    span = PAGE * N_HEADS_A
    tok = lax.broadcasted_iota(I32, (PAGE, span), 0)
    ln = lax.broadcasted_iota(I32, (PAGE, span), 1)
    expand = jnp.where(lax.shift_right_logical(ln, 3) == tok, 1.0, 0.0).astype(BF16)

    def valid_of(mk, width):
        e = _dot(mk.astype(BF16), expand[:, :width])
        sel = jnp.broadcast_to(e[:nq][:, None, :], (nq, N_HEADS_A, width)).reshape(rows, width) > 0.5
        lane_h = lax.broadcasted_iota(I32, (rows, width), 1) & (N_HEADS_A - 1)
        row_h = lax.broadcasted_iota(I32, (rows, width), 0) & (N_HEADS_A - 1)
        return jnp.logical_and(sel, lane_h == row_h)

    def update(kb, vb, valid, first):
        s = jnp.where(valid, _dot_nt(q, kb), NEG)
        m_blk = jnp.max(s, axis=1, keepdims=True)
        if first:
            m_new = m_blk
            p = jnp.where(valid, jnp.exp(s - m_new), 0.0)
            l_ref[...] = jnp.sum(p, axis=1, keepdims=True)
            acc_ref[...] = _dot(p.astype(BF16), vb)
        else:
            m_old = m_ref[...]
            m_new = jnp.maximum(m_old, m_blk)
            alpha = jnp.exp(m_old - m_new)
            p = jnp.where(valid, jnp.exp(s - m_new), 0.0)
            l_ref[...] = alpha * l_ref[...] + jnp.sum(p, axis=1, keepdims=True)
            acc_ref[...] = alpha * acc_ref[...] + _dot(p.astype(BF16), vb)
        m_ref[...] = m_new

    @pl.when(step == 0)
    def _():
        update(knew_ref[0], vnew_ref[0], valid_of(mnew_ref[0], PAGE), True)

    @pl.when(step > 0)
    def _():
        kb = jnp.concatenate([pg[0, 0].reshape(span, HEAD_A).astype(BF16) for pg in kpages], axis=0)
        vb = jnp.concatenate([pg[0, 0].reshape(span, HEAD_A).astype(BF16) for pg in vpages], axis=0)
        valid = jnp.concatenate([valid_of(mask_ref[0, :, u * PAGE:(u + 1) * PAGE], span)
                                 for u in range(ATT_PAGES)], axis=1)
        update(kb, vb, valid, False)

    @pl.when(step == pl.num_programs(1) - 1)
    def _():
        o_ref[0] = acc_ref[...] / l_ref[...]


def _attn_sample(page_table, layer, q, knew, vnew, mask, cache_k, cache_v):
    b, n_pages = page_table.shape
    rows = q.shape[1]
    nq = rows // N_HEADS_A
    past = n_pages * PAGE
    steps = n_pages // ATT_PAGES + 1
    pg = lambda s, u: jnp.maximum(s - 1, 0) * ATT_PAGES + u
    page_spec = lambda u: pl.BlockSpec((1, 1, PAGE, N_HEADS_A, HEAD_A),
                                       lambda bi, s, pt: (layer, pt[bi, pg(s, u)], 0, 0, 0))
    per_b = lambda shp: pl.BlockSpec((1,) + shp, lambda bi, s, pt: (bi, 0, 0))
    return pl.pallas_call(
        functools.partial(_attn_sample_kernel, nq=nq),
        out_shape=jax.ShapeDtypeStruct((b, rows, HEAD_A), F32),
        grid_spec=pltpu.PrefetchScalarGridSpec(
            num_scalar_prefetch=1,
            grid=(b, steps),
            in_specs=[per_b((rows, HEAD_A)), per_b((PAGE, HEAD_A)), per_b((PAGE, HEAD_A)),
                      pl.BlockSpec((1, MASK_ROWS, PAGE), lambda bi, s, pt: (bi, 0, past // PAGE)),
                      pl.BlockSpec((1, MASK_ROWS, ATT_PAGES * PAGE),
                                   lambda bi, s, pt: (bi, 0, jnp.maximum(s - 1, 0)))]
                     + [page_spec(u) for u in range(ATT_PAGES)] * 2,
            out_specs=per_b((rows, HEAD_A)),
            scratch_shapes=[pltpu.VMEM((rows, 1), F32), pltpu.VMEM((rows, 1), F32),
                            pltpu.VMEM((rows, HEAD_A), F32)],
        ),
        compiler_params=_cparams(("parallel", "arbitrary")),
        name="attn_sample",
    )(page_table, q, knew, vnew, mask, mask, *([cache_k] * ATT_PAGES), *([cache_v] * ATT_PAGES))


def _seg_sum(x, jmat):
    r = x.shape[0]
    xs = jnp.concatenate([x[:, p * LANES:(p + 1) * LANES] for p in range(N_PAIRS)], axis=0)
    hi, lo = _split_bf16(xs)
    ys = _dot(hi, jmat) + _dot(lo, jmat)
    return jnp.concatenate([ys[p * r:(p + 1) * r] for p in range(N_PAIRS)], axis=1)


def _rwkv_kernel(pb_ref, first_ref, s0_ref, mu_ref, w0_ref, a0_ref, kk_ref, ka_ref, rk_ref, lw_ref, lb_ref,
                 w2h, w2l, a2h, a2l, g2h, g2l, ob_ref, st_ref, carry_ref, state_ref, *, t_valid):
    nb, c = pb_ref.shape[0], pb_ref.shape[1]
    rows = nb * c
    n = pl.program_id(1)
    half = HEAD_B

    @pl.when(n == 0)
    def _():
        carry_ref[...] = first_ref[:, 0, :]
        z = jnp.zeros((half, half), F32)
        for b in range(nb):
            for p in range(N_PAIRS):
                top = jnp.concatenate([s0_ref[b, 2 * p], z], axis=1)
                bot = jnp.concatenate([z, s0_ref[b, 2 * p + 1]], axis=1)
                state_ref[b * N_PAIRS + p] = jnp.concatenate([top, bot], axis=0)

    pb = pb_ref[...].reshape(rows, PB_W)
    rowi = lax.broadcasted_iota(I32, (rows, 1), 0)
    step_i = rowi & (c - 1)
    carry = jnp.broadcast_to(carry_ref[...][:, None, :], (nb, c, PB_W)).reshape(rows, PB_W)
    prev = jnp.where(step_i == 0, carry, pltpu.roll(pb, 1, axis=0))
    carry_ref[...] = pb_ref[:, c - 1, :]
    xm = pb + (prev - pb) * mu_ref[...]

    r = xm[:, 0:C_B]
    k = xm[:, C_B:2 * C_B]
    v = xm[:, 2 * C_B:3 * C_B]
    xw = xm[:, 3 * C_B:3 * C_B + 128]
    xa = xm[:, 3 * C_B + 128:3 * C_B + 256]
    xg = xm[:, 3 * C_B + 256:3 * C_B + 512]
    w = -jax.nn.softplus(-(w0_ref[...] + _dot3w(jnp.tanh(xw), w2h, w2l))) - 0.5
    ld = -jnp.exp(w)
    a = jax.nn.sigmoid(a0_ref[...] + _dot3w(xa, a2h, a2l))
    g = _dot3w(jax.nn.sigmoid(xg), g2h, g2l)

    li = lax.broadcasted_iota(I32, (LANES, LANES), 0)
    lj = lax.broadcasted_iota(I32, (LANES, LANES), 1)
    same_head = (li // half) == (lj // half)
    jmat = jnp.where(same_head, 1.0, 0.0).astype(BF16)

    kkr = k * kk_ref[...]
    kk = kkr / jnp.maximum(jnp.sqrt(_seg_sum(kkr * kkr, jmat)), 1e-12)
    k2 = k * (1.0 + (a - 1.0) * ka_ref[...])
    bonus = _seg_sum(r * k2 * rk_ref[...], jmat)
    avec = -kk
    bvec = kk * a
    if t_valid < c:
        live = step_i < t_valid
        ld = jnp.where(live, ld, 0.0)
        avec = jnp.where(live, avec, 0.0)
        bvec = jnp.where(live, bvec, 0.0)
        k2s = jnp.where(live, k2, 0.0)
        vs = jnp.where(live, v, 0.0)
    else:
        k2s, vs = k2, v

    ti = lax.broadcasted_iota(I32, (rows, rows), 0)
    tj = lax.broadcasted_iota(I32, (rows, rows), 1)
    tri = jnp.where(jnp.logical_and(tj <= ti, (ti // c) == (tj // c)), 1.0, 0.0).astype(BF16)
    l1 = ld.astype(BF16)
    rem = ld - l1.astype(F32)
    l2 = rem.astype(BF16)
    l3 = (rem - l2.astype(F32)).astype(BF16)
    logc = _dot(tri, l1) + _dot(tri, l2) + _dot(tri, l3)
    logc_end = jnp.broadcast_to(logc.reshape(nb, c, C_B)[:, c - 1:c, :], (nb, c, C_B)).reshape(rows, C_B)
    e_pos = jnp.exp(logc)
    e_neg = jnp.exp(-logc)
    e_end = jnp.exp(logc_end - logc)

    def grp(x):
        return jnp.stack([x[b * c:(b + 1) * c, p * LANES:(p + 1) * LANES]
                          for b in range(nb) for p in range(N_PAIRS)], axis=0)

    rt = grp(r * e_pos)
    at = grp(avec * jnp.exp(logc - ld))
    kt = grp(k2s * e_neg)
    bt = grp(bvec * e_neg)
    kc = grp(k2s * e_end)
    bc = grp(bvec * e_end)
    vg = grp(vs)
    dec_end = grp(jnp.exp(logc_end))[:, 0:1, :]

    lane = lax.broadcasted_iota(I32, (1, 1, LANES), 2)
    m0 = lane < half
    col4 = lax.broadcasted_iota(I32, (1, c, 4 * c), 2) & (c - 1)
    row4 = lax.broadcasted_iota(I32, (1, c, 4 * c), 1)
    eye2 = jnp.where(lax.broadcasted_iota(I32, (1, 2 * c, 2 * c), 1)
                     == lax.broadcasted_iota(I32, (1, 2 * c, 2 * c), 2), 1.0, 0.0)
    lane2c = lax.broadcasted_iota(I32, (1, 1, 2 * c), 2) < c

    def two(x):
        return jnp.concatenate([jnp.where(m0, x, 0.0), jnp.where(m0, 0.0, x)], axis=1)

    s_old = state_ref[...]
    lhs1 = jnp.concatenate([rt, at], axis=1).astype(BF16)
    rhs1 = jnp.concatenate([two(kt), two(bt)], axis=1).astype(BF16)
    amat = _bmm_nt(lhs1, rhs1)
    a_r = jnp.where(col4 <= row4, amat[:, :c], 0.0)
    a_a = jnp.where(col4 < row4, amat[:, c:], 0.0)
    x0 = _bmm_nt(lhs1, s_old.astype(BF16))
    vm = two(vg).astype(BF16)
    x = x0[:, c:] + _bmm(a_a[:, :, :2 * c].astype(BF16), vm)
    lab = a_a[:, :, 2 * c:]
    lbd = jnp.concatenate([jnp.where(lane2c, lab, 0.0), jnp.where(lane2c, 0.0, lab)], axis=1)
    pm = eye2 + lbd
    lk = lbd
    for _ in range(int(np.log2(c)) - 1):
        lkb = lk.astype(BF16)
        lk = _bmm(lkb, lkb)
        pm = pm + _bmm(pm.astype(BF16), lk.astype(BF16))
    us = _bmm(pm.astype(BF16), two(x).astype(BF16))
    u = us[:, :c] + us[:, c:]
    y = x0[:, :c] + _bmm(a_r.astype(BF16), jnp.concatenate([vm, two(u).astype(BF16)], axis=1))
    vals = jnp.concatenate([vg, u], axis=1)
    keys = jnp.concatenate([kc, bc], axis=1).astype(BF16)
    add = _bmm(jnp.swapaxes(vals, 1, 2).astype(BF16), keys)
    state_ref[...] = s_old * dec_end + jnp.where(same_head[None], add, 0.0)

    y = jnp.concatenate([jnp.concatenate([y[b * N_PAIRS + p] for p in range(N_PAIRS)], axis=1)
                         for b in range(nb)], axis=0)
    mean = _seg_sum(y, jmat) * (1.0 / half)
    d = y - mean
    var = _seg_sum(d * d, jmat) * (1.0 / half)
    yn = d * lax.rsqrt(var + LNX_EPS) * lw_ref[...] + lb_ref[...]
    ob_ref[...] = ((yn + bonus * v) * g).astype(ob_ref.dtype).reshape(nb, c, C_B)

    @pl.when(n == pl.num_programs(1) - 1)
    def _():
        for b in range(nb):
            for p in range(N_PAIRS):
                s_new = state_ref[b * N_PAIRS + p]
                st_ref[b, 2 * p] = s_new[:half, :half]
                st_ref[b, 2 * p + 1] = s_new[half:, half:]


def _rwkv(pb, first, s0, vecs, loras, t_valid):
    b, t, _ = pb.shape
    nb = int(np.gcd(b, RWKV_NB))
    vec = lambda a: pl.BlockSpec(a.shape, lambda bi, n: (0, 0))
    st_spec = pl.BlockSpec((nb,) + s0.shape[1:], lambda bi, n: (bi, 0, 0, 0))
    return pl.pallas_call(
        functools.partial(_rwkv_kernel, t_valid=t_valid),
        out_shape=(jax.ShapeDtypeStruct((b, t, C_B), BF16), jax.ShapeDtypeStruct(s0.shape, F32)),
        grid=(b // nb, t // CHUNK),
        in_specs=[pl.BlockSpec((nb, CHUNK, PB_W), lambda bi, n: (bi, n, 0)),
                  pl.BlockSpec((nb, 1, PB_W), lambda bi, n: (bi, 0, 0)), st_spec]
                 + [vec(a) for a in vecs] + [vec(a) for a in loras],
        out_specs=(pl.BlockSpec((nb, CHUNK, C_B), lambda bi, n: (bi, n, 0)), st_spec),
        scratch_shapes=[pltpu.VMEM((nb, PB_W), F32), pltpu.VMEM((nb * N_PAIRS, LANES, LANES), F32)],
        compiler_params=_cparams(("parallel", "arbitrary")),
        name="rwkv7",
    )(pb, first, s0, *vecs, *loras)


def _merge_kernel(oa_ref, ob_ref, wpa_ref, wpb_ref, ga_ref, gb_ref, o_ref):
    za = _dot(oa_ref[...], wpa_ref[...])
    zb = _dot(ob_ref[...], wpb_ref[...])
    o_ref[...] = (ga_ref[...].astype(F32) * za + gb_ref[...].astype(F32) * zb).astype(o_ref.dtype)


def _merge(oa, ob, wpa, wpb, gates, tn=1024):
    m = oa.shape[0]
    tm = _row_tile(m, 1024)
    nb = D_MODEL // tn
    return pl.pallas_call(
        _merge_kernel,
        out_shape=jax.ShapeDtypeStruct((m, D_MODEL), BF16),
        grid=(m // tm, nb),
        in_specs=[pl.BlockSpec((tm, C_A), lambda i, j: (i, 0)), pl.BlockSpec((tm, C_B), lambda i, j: (i, 0)),
                  pl.BlockSpec((C_A, tn), lambda i, j: (0, j)), pl.BlockSpec((C_B, tn), lambda i, j: (0, j)),
                  pl.BlockSpec((tm, tn), lambda i, j: (i, j)), pl.BlockSpec((tm, tn), lambda i, j: (i, j + nb))],
        out_specs=pl.BlockSpec((tm, tn), lambda i, j: (i, j)),
        compiler_params=_cparams(("parallel", "parallel")),
        name="merge",
    )(oa, ob, wpa, wpb, gates, gates)


def _oproj_kernel(x_ref, mix_ref, wo_ref, g_ref, x1_ref, h2_ref):
    x1 = x_ref[...] + _dot(mix_ref[...], wo_ref[...])
    x1_ref[...] = x1
    h2 = x1 * lax.rsqrt(jnp.mean(x1 * x1, axis=-1, keepdims=True) + NORM_EPS) * g_ref[...]
    h2_ref[...] = h2.astype(BF16)


def _oproj(x, mixed, wo, g):
    m, d = x.shape
    tm = _row_tile(m)
    row = pl.BlockSpec((tm, d), lambda i: (i, 0))
    return pl.pallas_call(
        _oproj_kernel,
        out_shape=(jax.ShapeDtypeStruct((m, d), F32), jax.ShapeDtypeStruct((m, d), BF16)),
        grid=(m // tm,),
        in_specs=[row, row, pl.BlockSpec((d, d), lambda i: (0, 0)), pl.BlockSpec((1, d), lambda i: (0, 0))],
        out_specs=(row, row),
        compiler_params=_cparams(("parallel",)),
        name="oproj_norm",
    )(x, mixed, wo, g.reshape(1, d))


def _ffn_up_kernel(h_ref, wg_ref, wu_ref, o_ref):
    h = h_ref[...]
    gate = _dot(h, wg_ref[...])
    up = _dot(h, wu_ref[...])
    o_ref[...] = (gate * jax.nn.sigmoid(gate) * up).astype(o_ref.dtype)


def _ffn_up(h2, wg, wu, tn=512):
    m, d = h2.shape
    n = wg.shape[1]
    tm = _row_tile(m, 1024)
    return pl.pallas_call(
        _ffn_up_kernel,
        out_shape=jax.ShapeDtypeStruct((m, n), BF16),
        grid=(m // tm, n // tn),
        in_specs=[pl.BlockSpec((tm, d), lambda i, j: (i, 0)), pl.BlockSpec((d, tn), lambda i, j: (0, j)),
                  pl.BlockSpec((d, tn), lambda i, j: (0, j))],
        out_specs=pl.BlockSpec((tm, tn), lambda i, j: (i, j)),
        compiler_params=_cparams(("parallel", "parallel")),
        name="ffn_up",
    )(h2, wg, wu)


def _ffn_down_kernel(x_ref, a_ref, wd_ref, o_ref):
    o_ref[...] = x_ref[...] + _dot(a_ref[...], wd_ref[...])


def _ffn_down(x1, act, wd, tn=1024):
    m, d = x1.shape
    kdim = act.shape[1]
    tm = _row_tile(m)
    return pl.pallas_call(
        _ffn_down_kernel,
        out_shape=jax.ShapeDtypeStruct((m, d), F32),
        grid=(m // tm, d // tn),
        in_specs=[pl.BlockSpec((tm, tn), lambda i, j: (i, j)), pl.BlockSpec((tm, kdim), lambda i, j: (i, 0)),
                  pl.BlockSpec((kdim, tn), lambda i, j: (0, j))],
        out_specs=pl.BlockSpec((tm, tn), lambda i, j: (i, j)),
        compiler_params=_cparams(("parallel", "parallel")),
        name="ffn_down",
    )(x1, act, wd)


def _pad_cols(w, width):
    return jnp.pad(w, ((0, 0), (0, width - w.shape[1])))


def _pad_rows(w, height):
    return jnp.pad(w, ((0, height - w.shape[0]), (0, 0)))


def _prep_weights(w_in, mu_b, w2, a2, g2):
    o = 0
    cols = {}
    for name, wdt in (("q", C_A), ("k", C_A), ("v", C_A), ("qi", IDX_HEADS * IDX_DIM), ("ki", IDX_DIM),
                      ("wi", IDX_HEADS), ("g", 2 * D_MODEL), ("rkv", 3 * C_B), ("xw", D_DECAY), ("xa", D_AAA),
                      ("xg", D_GATE)):
        cols[name] = (o, o + wdt)
        o += wdt
    cut = lambda name: w_in[:, cols[name][0]:cols[name][1]]
    w_idx = _pad_cols(jnp.concatenate([cut("qi"), cut("ki"), cut("wi")], axis=1), IDX_W)
    idx_hi, idx_lo = _split_bf16(w_idx)
    w_b = jnp.concatenate([cut("rkv"), _pad_cols(cut("xw"), 128), _pad_cols(cut("xa"), 128),
                           _pad_cols(cut("xg"), 256)], axis=1)
    nb0 = cols["rkv"][0]
    mu = lambda name: mu_b[cols[name][0] - nb0:cols[name][1] - nb0][None, :]
    mu_p = jnp.concatenate([mu("rkv"), _pad_cols(mu("xw"), 128), _pad_cols(mu("xa"), 128),
                            _pad_cols(mu("xg"), 256)], axis=1)
    loras = []
    for wl, hgt in ((w2, 128), (a2, 128), (g2, 256)):
        loras += list(_split_bf16(_pad_rows(wl, hgt)))
    return dict(q=cut("q").astype(BF16), k=cut("k").astype(BF16), v=cut("v").astype(BF16),
                idx_hi=idx_hi, idx_lo=idx_lo, g=cut("g").astype(BF16), b=w_b.astype(BF16), mu=mu_p,
                loras=loras)


def _layer(x, pos, wts, p, *, seq, attn_fn, first_pb, s0, t_valid):
    m = x.shape[0]
    nb = m // seq
    hh, hl = _norm_split(x, p["g_mix"])
    tabs_a = _rope_tables(pos, HEAD_A)
    tabs_i = _rope_tables(pos, IDX_DIM)
    (q_b,) = _proj(hh, wts["q"], tabs_a, rope=True, want_f32=False, want_bf16=True, scale=HEAD_A ** -0.5)
    k_f, k_b = _proj(hh, wts["k"], tabs_a, rope=True, want_f32=True, want_bf16=True)
    v_f, v_b = _proj(hh, wts["v"], tabs_a, rope=False, want_f32=True, want_bf16=True)
    qcat, kcat, ki_f, wi = _idx_proj(hh, hl, wts["idx_hi"], wts["idx_lo"], tabs_i)
    gates = _mm(hh, wts["g"], tn=1024, out_dtype=BF16, act="sigmoid")
    pb = _mm(hh, wts["b"], tn=PB_W // 4, out_dtype=F32)

    o_a = attn_fn(q_b, k_b, v_b, qcat, kcat, ki_f, wi)

    t_pad = -(-seq // CHUNK) * CHUNK
    pb3 = pb.reshape(nb, seq, PB_W)
    if t_pad != seq:
        pb3 = jnp.pad(pb3, ((0, 0), (0, t_pad - seq), (0, 0)))
    vecs = [wts["mu"]] + [p[n].reshape(1, C_B) for n in ("w0", "a0", "k_k", "k_a", "r_k", "lnx_w", "lnx_b")]
    o_b, wkv = _rwkv(pb3, first_pb, s0, vecs, wts["loras"], t_valid)
    o_b = o_b[:, :seq].reshape(m, C_B)

    mixed = _merge(o_a, o_b, p["w_pa"], p["w_pb"], gates)
    x1, h2 = _oproj(x, mixed, p["w_o"], p["g_ffn"])
    act = _ffn_up(h2, p["w_gate"], p["w_up"])
    x2 = _ffn_down(x1, act, p["w_down"])
    shift = _norm_f32(x.reshape(nb, seq, D_MODEL)[:, -1], p["g_mix"])
    return x2, k_f, v_f, ki_f, wkv, shift


def kernel(x_prompt, x_sample, cache_k, cache_v, cache_kidx, state_wkv, state_shift, page_table, g_mix, w_in, mu_b, w0, w2, a0, a2, g2, k_k, k_a, r_k, lnx_w, lnx_b, w_pa, w_pb, w_o, g_ffn, w_gate, w_up, w_down, g_final):
    bp, sp, d = x_prompt.shape
    bs, ts, _ = x_sample.shape
    depth = w_in.shape[0]
    n_pages = page_table.shape[1]
    past = n_pages * PAGE
    xp = x_prompt.reshape(bp * sp, d)
    xs = x_sample.reshape(bs * ts, d)
    outs_p, outs_s = [], []
    for l in range(depth):
        wts = _prep_weights(w_in[l], mu_b[l], w2[l], a2[l], g2[l])
        p = dict(g_mix=g_mix[l], w0=w0[l], a0=a0[l], k_k=k_k[l], k_a=k_a[l], r_k=r_k[l], lnx_w=lnx_w[l],
                 lnx_b=lnx_b[l], w_pa=w_pa[l].astype(BF16), w_pb=w_pb[l].astype(BF16), w_o=w_o[l].astype(BF16),
                 g_ffn=g_ffn[l], w_gate=w_gate[l].astype(BF16), w_up=w_up[l].astype(BF16),
                 w_down=w_down[l].astype(BF16))

        def attn_p(q_b, k_b, v_b, qcat, kcat, ki_f, wi):
            r3 = lambda a: a.reshape(bp, sp, a.shape[-1])
            return _attn_prompt(r3(q_b), r3(k_b), r3(v_b), r3(qcat), r3(kcat), r3(wi)).reshape(bp * sp, C_A)

        def attn_s(q_b, k_b, v_b, qcat, kcat, ki_f, wi):
            qc = qcat.reshape(bs, ts * IDX_HEADS, 256)
            qh, ql = qc[..., :IDX_DIM], qc[..., IDX_DIM:2 * IDX_DIM]
            wrow = wi.reshape(bs, ts * IDX_HEADS, 1)
            ki_new = jnp.pad(ki_f.reshape(bs, ts, IDX_DIM), ((0, 0), (0, PAGE - ts), (0, 0)))
            mask = _sel_sample(page_table, l, qh, ql, wrow, ki_new, cache_kidx, ts)
            as_rows = lambda a: a.reshape(bs, ts * N_HEADS_A, HEAD_A)
            pad_rows = lambda a: jnp.pad(as_rows(a), ((0, 0), (0, PAGE - ts * N_HEADS_A), (0, 0)))
            o = _attn_sample(page_table, l, as_rows(q_b), pad_rows(k_b), pad_rows(v_b), mask, cache_k, cache_v)
            return o.reshape(bs * ts, C_A).astype(BF16)

        zero_first = jnp.zeros((bp, 1, PB_W), F32)
        zero_state = jnp.zeros((bp, N_HEADS_B, HEAD_B, HEAD_B), F32)
        xp, kp, vp, kip, wkvp, shp = _layer(xp, jnp.arange(sp), wts, p, seq=sp, attn_fn=attn_p,
                                            first_pb=zero_first, s0=zero_state, t_valid=CHUNK)
        first_s = _mm(state_shift[l].astype(BF16), wts["b"], tn=PB_W // 4, out_dtype=F32).reshape(bs, 1, PB_W)
        pos_s = jnp.tile(past + jnp.arange(ts), bs)
        xs, ksm, vsm, kis, wkvs, shs = _layer(xs, pos_s, wts, p, seq=ts, attn_fn=attn_s,
                                              first_pb=first_s, s0=state_wkv[l], t_valid=ts)
        outs_p.append((kp.reshape(bp, sp, N_HEADS_A, HEAD_A), vp.reshape(bp, sp, N_HEADS_A, HEAD_A),
                       kip.reshape(bp, sp, IDX_DIM), wkvp, shp))
        outs_s.append((ksm.reshape(bs, ts, N_HEADS_A, HEAD_A), vsm.reshape(bs, ts, N_HEADS_A, HEAD_A),
                       kis.reshape(bs, ts, IDX_DIM), wkvs, shs))
    y_prompt = _norm_f32(xp, g_final).reshape(bp, sp, d)
    y_sample = _norm_f32(xs, g_final).reshape(bs, ts, d)
    stack = lambda outs, i: jnp.stack([o[i] for o in outs])
    return ((y_prompt, y_sample) + tuple(stack(outs_p, i) for i in range(5))
            + tuple(stack(outs_s, i) for i in range(5)))
```

```python
import functools

import numpy as np
import jax
import jax.numpy as jnp
from jax import lax
from jax.experimental import pallas as pl
from jax.experimental.pallas import tpu as pltpu

F32 = jnp.float32
BF16 = jnp.bfloat16
I32 = jnp.int32

D_MODEL = 2048
HEAD_A = 128
N_HEADS_A = 8
C_A = N_HEADS_A * HEAD_A
IDX_HEADS = 8
IDX_DIM = 64
TOPK = 256
HEAD_B = 64
N_HEADS_B = 16
C_B = N_HEADS_B * HEAD_B
D_DECAY = 64
D_AAA = 64
D_GATE = 160
LNX_EPS = 64e-5
ROPE_THETA = 500000.0
NORM_EPS = 1e-6
PAGE = 128
NEG = -1e30
INT_MIN = -(2 ** 31)

LANES = 128
VMEM_LIMIT = 56 * 1024 * 1024
CHUNK = 64
RWKV_NB = 4
N_PAIRS = N_HEADS_B // 2
TQ = 256
RANK_BLK = 256
MASK_ROWS = 16
PB_W = 3 * C_B + 128 + 128 + 256
IDX_W = 640


def _cparams(sem):
    return pltpu.CompilerParams(dimension_semantics=sem, vmem_limit_bytes=VMEM_LIMIT)


def _split_bf16(x):
    hi = x.astype(BF16)
    lo = (x - hi.astype(F32)).astype(BF16)
    return hi, lo


def _dot(a, b):
    return jnp.dot(a, b, preferred_element_type=F32)


def _dot_nt(a, b):
    return lax.dot_general(a, b, (((1,), (1,)), ((), ())), preferred_element_type=F32)


def _bmm(a, b):
    return lax.dot_general(a, b, (((2,), (1,)), ((0,), (0,))), preferred_element_type=F32)


def _bmm_nt(a, b):
    return lax.dot_general(a, b, (((2,), (2,)), ((0,), (0,))), preferred_element_type=F32)


def _dot3w(x, wh_ref, wl_ref):
    xh, xl = _split_bf16(x)
    wh = wh_ref[...]
    return _dot(xh, wh) + _dot(xl, wh) + _dot(xh, wl_ref[...])


def _norm_split_kernel(x_ref, g_ref, hi_ref, lo_ref):
    x = x_ref[...]
    y = x * lax.rsqrt(jnp.mean(x * x, axis=-1, keepdims=True) + NORM_EPS) * g_ref[...]
    hi = y.astype(BF16)
    hi_ref[...] = hi
    lo_ref[...] = (y - hi.astype(F32)).astype(BF16)


def _norm_f32_kernel(x_ref, g_ref, o_ref):
    x = x_ref[...]
    o_ref[...] = x * lax.rsqrt(jnp.mean(x * x, axis=-1, keepdims=True) + NORM_EPS) * g_ref[...]


def _row_tile(m, pref=512):
    for tm in (pref, 512):
        if m % tm == 0:
            return tm
    return m


def _norm_split(x, g):
    m, d = x.shape
    tm = _row_tile(m)
    return pl.pallas_call(
        _norm_split_kernel,
        out_shape=(jax.ShapeDtypeStruct((m, d), BF16), jax.ShapeDtypeStruct((m, d), BF16)),
        grid=(m // tm,),
        in_specs=[pl.BlockSpec((tm, d), lambda i: (i, 0)), pl.BlockSpec((1, d), lambda i: (0, 0))],
        out_specs=(pl.BlockSpec((tm, d), lambda i: (i, 0)), pl.BlockSpec((tm, d), lambda i: (i, 0))),
        compiler_params=_cparams(("parallel",)),
        name="norm_split",
    )(x, g.reshape(1, d))


def _norm_f32(x, g):
    m, d = x.shape
    tm = _row_tile(m)
    return pl.pallas_call(
        _norm_f32_kernel,
        out_shape=jax.ShapeDtypeStruct((m, d), F32),
        grid=(m // tm,),
        in_specs=[pl.BlockSpec((tm, d), lambda i: (i, 0)), pl.BlockSpec((1, d), lambda i: (0, 0))],
        out_specs=pl.BlockSpec((tm, d), lambda i: (i, 0)),
        compiler_params=_cparams(("parallel",)),
        name="norm_f32",
    )(x, g.reshape(1, d))


def _rope_tables(pos, head_dim):
    rot = head_dim // 4
    half = rot // 2
    inv = ROPE_THETA ** (-jnp.arange(half, dtype=F32) * 2.0 / rot)
    ang = pos.astype(F32)[:, None] * inv[None, :]
    cos, sin = jnp.cos(ang), jnp.sin(ang)
    t = pos.shape[0]
    c = jnp.ones((t, head_dim), F32).at[:, :half].set(cos).at[:, half:rot].set(cos)
    s1 = jnp.zeros((t, head_dim), F32).at[:, half:rot].set(sin)
    s2 = jnp.zeros((t, head_dim), F32).at[:, :half].set(-sin)
    rep = LANES // head_dim
    return tuple(jnp.tile(a, (1, rep)) for a in (c, s1, s2))


def _rope_lanes(x, c, s1, s2, half):
    return x * c + pltpu.roll(x, half, axis=1) * s1 + pltpu.roll(x, LANES - half, axis=1) * s2


def _proj_kernel(a_ref, w_ref, c_ref, s1_ref, s2_ref, *out_refs, rope, want_f32, want_bf16, scale):
    acc = _dot(a_ref[...], w_ref[...])
    if rope:
        c, s1, s2 = c_ref[...], s1_ref[...], s2_ref[...]
        parts = [_rope_lanes(acc[:, h * LANES:(h + 1) * LANES], c, s1, s2, HEAD_A // 8)
                 for h in range(acc.shape[1] // LANES)]
        acc = jnp.concatenate(parts, axis=1)
    if scale is not None:
        acc = acc * scale
    k = 0
    if want_f32:
        out_refs[k][...] = acc
        k += 1
    if want_bf16:
        out_refs[k][...] = acc.astype(BF16)


def _proj(a, w, tabs, *, rope, want_f32, want_bf16, scale=None):
    m, kdim = a.shape
    n = w.shape[1]
    tm = _row_tile(m)
    ntab = tabs[0].shape[0] // tm
    outs, ospecs = [], []
    for want, dt in ((want_f32, F32), (want_bf16, BF16)):
        if want:
            outs.append(jax.ShapeDtypeStruct((m, n), dt))
            ospecs.append(pl.BlockSpec((tm, n), lambda i: (i, 0)))
    tab_spec = pl.BlockSpec((tm, LANES), lambda i: (i % ntab, 0))
    res = pl.pallas_call(
        functools.partial(_proj_kernel, rope=rope, want_f32=want_f32, want_bf16=want_bf16, scale=scale),
        out_shape=tuple(outs),
        grid=(m // tm,),
        in_specs=[pl.BlockSpec((tm, kdim), lambda i: (i, 0)), pl.BlockSpec((kdim, n), lambda i: (0, 0)),
                  tab_spec, tab_spec, tab_spec],
        out_specs=tuple(ospecs),
        compiler_params=_cparams(("parallel",)),
        name="proj_rope" if rope else "proj",
    )(a, w, *tabs)
    return res


def _mm_kernel(a_ref, w_ref, o_ref, *, act):
    acc = _dot(a_ref[...], w_ref[...])
    if act == "sigmoid":
        acc = jax.nn.sigmoid(acc)
    o_ref[...] = acc.astype(o_ref.dtype)


def _mm(a, w, *, tn, out_dtype, act=None):
    m, kdim = a.shape
    n = w.shape[1]
    tm = _row_tile(m, 1024)
    return pl.pallas_call(
        functools.partial(_mm_kernel, act=act),
        out_shape=jax.ShapeDtypeStruct((m, n), out_dtype),
        grid=(m // tm, n // tn),
        in_specs=[pl.BlockSpec((tm, kdim), lambda i, j: (i, 0)), pl.BlockSpec((kdim, tn), lambda i, j: (0, j))],
        out_specs=pl.BlockSpec((tm, tn), lambda i, j: (i, j)),
        compiler_params=_cparams(("parallel", "parallel")),
        name="mm_" + (act or "plain"),
    )(a, w)


def _idx_proj_kernel(hh_ref, hl_ref, wh_ref, wl_ref, c_ref, s1_ref, s2_ref,
                     qcat_ref, kcat_ref, kf_ref, wi_ref):
    hh = hh_ref[...]
    wh = wh_ref[...]
    acc = _dot(hh, wh) + _dot(hl_ref[...], wh) + _dot(hh, wl_ref[...])
    c, s1, s2 = c_ref[...], s1_ref[...], s2_ref[...]
    half = IDX_DIM // 8
    zeros = jnp.zeros((acc.shape[0], IDX_DIM), F32)

    def hi_lo(x):
        hi = x.astype(BF16).astype(F32)
        return hi, (x - hi).astype(BF16).astype(F32)

    pieces = []
    for g in range(IDX_HEADS * IDX_DIM // LANES):
        blk = _rope_lanes(acc[:, g * LANES:(g + 1) * LANES], c, s1, s2, half)
        for u in range(LANES // IDX_DIM):
            hi, lo = hi_lo(blk[:, u * IDX_DIM:(u + 1) * IDX_DIM])
            pieces += [hi, lo, hi, zeros]
    qcat_ref[...] = jnp.concatenate(pieces, axis=1).astype(BF16)
    last = acc[:, IDX_HEADS * IDX_DIM:]
    ki = _rope_lanes(last, c, s1, s2, half)[:, :IDX_DIM]
    kf_ref[...] = ki
    hi, lo = hi_lo(ki)
    kcat_ref[...] = jnp.concatenate([hi, hi, lo, zeros], axis=1).astype(BF16)
    wi_ref[...] = last[:, IDX_DIM:IDX_DIM + IDX_HEADS] * (IDX_HEADS ** -0.5 * IDX_DIM ** -0.5)


def _idx_proj(hh, hl, wh, wl, tabs):
    m, d = hh.shape
    tm = _row_tile(m)
    ntab = tabs[0].shape[0] // tm
    row = lambda w: pl.BlockSpec((tm, w), lambda i: (i, 0))
    full = lambda a: pl.BlockSpec(a.shape, lambda i: (0, 0))
    tab_spec = pl.BlockSpec((tm, LANES), lambda i: (i % ntab, 0))
    return pl.pallas_call(
        _idx_proj_kernel,
        out_shape=(jax.ShapeDtypeStruct((m, IDX_HEADS * 256), BF16), jax.ShapeDtypeStruct((m, 256), BF16),
                   jax.ShapeDtypeStruct((m, IDX_DIM), F32), jax.ShapeDtypeStruct((m, IDX_HEADS), F32)),
        grid=(m // tm,),
        in_specs=[row(d), row(d), full(wh), full(wl), tab_spec, tab_spec, tab_spec],
        out_specs=(row(IDX_HEADS * 256), row(256), row(IDX_DIM), row(IDX_HEADS)),
        compiler_params=_cparams(("parallel",)),
        name="idx_proj",
    )(hh, hl, wh, wl, *tabs)


def _sort_key(s):
    bits = lax.bitcast_convert_type(s, I32)
    return bits ^ (lax.shift_right_arithmetic(bits, 31) & 0x7FFFFFFF)


def _kth_largest_key(key, n_sel):
    r = key.shape[0]

    def body(it, prefix):
        cand_u = prefix | lax.shift_left(jnp.int32(1), 31 - it)
        cand_s = cand_u ^ INT_MIN
        cnt = jnp.sum(jnp.where(key >= cand_s, 1.0, 0.0), axis=1, keepdims=True)
        return jnp.where(cnt >= n_sel, cand_u, prefix)

    return lax.fori_loop(0, 32, body, jnp.zeros((r, 1), I32)) ^ INT_MIN


def _tie_select(key, thr, n_sel, blk):
    r, l = key.shape
    gt = key > thr
    eq = key == thr
    need = n_sel - jnp.sum(jnp.where(gt, 1.0, 0.0), axis=1, keepdims=True)
    rr = lax.broadcasted_iota(I32, (blk, blk), 0)
    cc = lax.broadcasted_iota(I32, (blk, blk), 1)
    upper = jnp.where(rr < cc, 1.0, 0.0).astype(BF16)
    offs = jnp.zeros((r, 1), F32)
    parts = []
    for b in range(l // blk):
        e = jnp.where(eq[:, b * blk:(b + 1) * blk], 1.0, 0.0)
        rank = _dot(e.astype(BF16), upper) + offs
        parts.append(jnp.logical_and(eq[:, b * blk:(b + 1) * blk], rank < need))
        offs = offs + jnp.sum(e, axis=1, keepdims=True)
    return jnp.logical_or(gt, jnp.concatenate(parts, axis=1))


def _selection_bias(s, causal, n_sel, blk, bias_ref):
    key = _sort_key(s)
    thr = _kth_largest_key(key, n_sel)
    ge_c = jnp.logical_and(key >= thr, causal)
    cnt = (jnp.sum(jnp.where(ge_c, 1.0, 0.0), axis=1, keepdims=True)
           + jnp.sum(jnp.where(jnp.logical_or(key <= thr, causal), 0.0, 1.0), axis=1, keepdims=True))
    bias_ref[...] = jnp.where(ge_c, 0.0, NEG)

    @pl.when(jnp.max(cnt) > n_sel)
    def _():
        sel = _tie_select(key, thr, n_sel, blk)
        bias_ref[...] = jnp.where(jnp.logical_and(sel, causal), 0.0, NEG)


def _attn_prompt_kernel(q_ref, k_ref, v_ref, qi_ref, ki_ref, wi_ref, o_ref, bias_ref, *, n_sel, q0):
    tq = q_ref.shape[1]
    width = k_ref.shape[1]
    ki = ki_ref[0]
    wi = wi_ref[0]
    score = jnp.zeros((tq, width), F32)
    for h in range(IDX_HEADS):
        lg = _dot_nt(qi_ref[0, :, h * 256:(h + 1) * 256], ki)
        score = score + wi[:, h:h + 1] * jnp.maximum(lg, 0.0)
    col = lax.broadcasted_iota(I32, (tq, width), 1)
    row = q0 + lax.broadcasted_iota(I32, (tq, width), 0)
    causal = col <= row
    _selection_bias(jnp.where(causal, score, NEG), causal, n_sel, RANK_BLK, bias_ref)
    bias = bias_ref[...]
    for h in range(N_HEADS_A):
        sl = slice(h * HEAD_A, (h + 1) * HEAD_A)
        s = _dot_nt(q_ref[0, :, sl], k_ref[0, :, sl]) + bias
        m = jnp.max(s, axis=1, keepdims=True)
        p = jnp.exp(s - m)
        l = jnp.sum(p, axis=1, keepdims=True)
        o = _dot(p.astype(BF16), v_ref[0, :, sl]) / l
        o_ref[0, :, sl] = o.astype(o_ref.dtype)


def _attn_prompt(q, k, v, qcat, kcat, wi):
    b, s_len, _ = q.shape
    tq = min(TQ, s_len)
    n_sel = min(TOPK, s_len // 4)
    outs = []
    for i in range(s_len // tq):
        width = (i + 1) * tq
        blk_q = lambda w, i=i: pl.BlockSpec((1, tq, w), lambda bi: (bi, i, 0))
        blk_k = lambda w, width=width: pl.BlockSpec((1, width, w), lambda bi: (bi, 0, 0))
        outs.append(pl.pallas_call(
            functools.partial(_attn_prompt_kernel, n_sel=n_sel, q0=i * tq),
            out_shape=jax.ShapeDtypeStruct((b, tq, C_A), BF16),
            grid=(b,),
            in_specs=[blk_q(C_A), blk_k(C_A), blk_k(C_A), blk_q(IDX_HEADS * 256), blk_k(256), blk_q(IDX_HEADS)],
            out_specs=pl.BlockSpec((1, tq, C_A), lambda bi: (bi, 0, 0)),
            scratch_shapes=[pltpu.VMEM((tq, width), F32)],
            compiler_params=_cparams(("parallel",)),
            name="attn_prompt_%d" % i,
        )(q, k, v, qcat, kcat, wi))
    return jnp.concatenate(outs, axis=1)


SEL_PAGES = 16
ATT_PAGES = 8


def _score_sample_kernel(pt_ref, qh_ref, ql_ref, w_ref, knew_ref, *rest, past):
    pages = rest[:SEL_PAGES]
    sc_ref = rest[SEL_PAGES]
    step = pl.program_id(1)
    qh, ql, w = qh_ref[0], ql_ref[0], w_ref[0]
    nq = qh.shape[0] // IDX_HEADS

    def scores_of(kt):
        kh, kl = _split_bf16(kt)
        lg = _dot(qh, kh) + _dot(ql, kh) + _dot(qh, kl)
        x = w * jnp.maximum(lg, 0.0)
        return jnp.sum(x.reshape(nq, IDX_HEADS, kt.shape[1]), axis=1)

    @pl.when(step == 0)
    def _():
        sc_ref[0] = jnp.zeros(sc_ref.shape[1:], F32)
        sc_ref[0, 0:nq, past:past + PAGE] = scores_of(knew_ref[0])

    span = SEL_PAGES * PAGE
    off = pl.multiple_of(step * span, span)
    sc_ref[0, 0:nq, pl.ds(off, span)] = scores_of(jnp.concatenate([pg[0, 0] for pg in pages], axis=1))


def _select_sample_kernel(s_ref, mask_ref, bias_ref, *, n_sel, past, t_new, nq):
    s = s_ref[...]
    col = lax.broadcasted_iota(I32, s.shape, 1)
    qpos = past + (lax.broadcasted_iota(I32, s.shape, 0) & (nq - 1))
    causal = col <= qpos
    s = jnp.where(causal, s, NEG)
    s = jnp.where(col < past + t_new, s, -jnp.inf)
    _selection_bias(s, causal, n_sel, LANES, bias_ref)
    mask_ref[...] = jnp.where(bias_ref[...] > -1.0, 1.0, 0.0)


def _sel_sample(page_table, layer, qh, ql, w, knew, cache_ki, t_new):
    b, n_pages = page_table.shape
    past = n_pages * PAGE
    n_sel = min(TOPK, (past + t_new) // 4)
    rows = qh.shape[1]
    nq = rows // IDX_HEADS
    assert nq & (nq - 1) == 0
    width = past + PAGE
    steps = n_pages // SEL_PAGES
    cache_ki = jnp.swapaxes(cache_ki, 2, 3)
    knew = jnp.swapaxes(knew, 1, 2)
    page_spec = lambda u: pl.BlockSpec((1, 1, IDX_DIM, PAGE),
                                       lambda bi, s, pt: (layer, pt[bi, s * SEL_PAGES + u], 0, 0))
    per_b = lambda shp: pl.BlockSpec((1,) + shp, lambda bi, s, pt: (bi, 0, 0))
    scores = pl.pallas_call(
        functools.partial(_score_sample_kernel, past=past),
        out_shape=jax.ShapeDtypeStruct((b, 8, width), F32),
        grid_spec=pltpu.PrefetchScalarGridSpec(
            num_scalar_prefetch=1,
            grid=(b, steps),
            in_specs=[per_b((rows, IDX_DIM)), per_b((rows, IDX_DIM)), per_b((rows, 1)), per_b((IDX_DIM, PAGE))]
                     + [page_spec(u) for u in range(SEL_PAGES)],
            out_specs=per_b((8, width)),
        ),
        compiler_params=_cparams(("parallel", "arbitrary")),
        name="score_sample",
    )(page_table, qh, ql, w, knew, *([cache_ki] * SEL_PAGES))
    flat = scores[:, :nq].reshape(b * nq, width)
    mask = pl.pallas_call(
        functools.partial(_select_sample_kernel, n_sel=n_sel, past=past, t_new=t_new, nq=nq),
        out_shape=jax.ShapeDtypeStruct((b * nq, width), F32),
        scratch_shapes=[pltpu.VMEM((b * nq, width), F32)],
        compiler_params=pltpu.CompilerParams(vmem_limit_bytes=VMEM_LIMIT),
        name="select_sample",
    )(flat)
    return jnp.pad(mask.reshape(b, nq, width), ((0, 0), (0, MASK_ROWS - nq), (0, 0)))


def _attn_sample_kernel(pt_ref, q_ref, knew_ref, vnew_ref, mnew_ref, mask_ref, *rest, nq):
    kpages = rest[:ATT_PAGES]
    vpages = rest[ATT_PAGES:2 * ATT_PAGES]
    o_ref = rest[2 * ATT_PAGES]
    m_ref, l_ref, acc_ref = rest[2 * ATT_PAGES + 1:]
    step = pl.program_id(1)
    q = q_ref[0]
    rows = q.shape[0]
    span = PAGE * N_HEADS_A
    tok = lax.broadcasted_iota(I32, (PAGE, span), 0)
    ln = lax.broadcasted_iota(I32, (PAGE, span), 1)
    expand = jnp.where(lax.shift_right_logical(ln, 3) == tok, 1.0, 0.0).astype(BF16)

    def valid_of(mk, width):
        e = _dot(mk.astype(BF16), expand[:, :width])
        sel = jnp.broadcast_to(e[:nq][:, None, :], (nq, N_HEADS_A, width)).reshape(rows, width) > 0.5
        lane_h = lax.broadcasted_iota(I32, (rows, width), 1) & (N_HEADS_A - 1)
        row_h = lax.broadcasted_iota(I32, (rows, width), 0) & (N_HEADS_A - 1)
        return jnp.logical_and(sel, lane_h == row_h)

    def update(kb, vb, valid, first):
        s = jnp.where(valid, _dot_nt(q, kb), NEG)
        m_blk = jnp.max(s, axis=1, keepdims=True)
        if first:
            m_new = m_blk
            p = jnp.where(valid, jnp.exp(s - m_new), 0.0)
            l_ref[...] = jnp.sum(p, axis=1, keepdims=True)
            acc_ref[...] = _dot(p.astype(BF16), vb)
        else:
            m_old = m_ref[...]
            m_new = jnp.maximum(m_old, m_blk)
            alpha = jnp.exp(m_old - m_new)
            p = jnp.where(valid, jnp.exp(s - m_new), 0.0)
            l_ref[...] = alpha * l_ref[...] + jnp.sum(p, axis=1, keepdims=True)
            acc_ref[...] = alpha * acc_ref[...] + _dot(p.astype(BF16), vb)
        m_ref[...] = m_new

    @pl.when(step == 0)
    def _():
        update(knew_ref[0], vnew_ref[0], valid_of(mnew_ref[0], PAGE), True)

    @pl.when(step > 0)
    def _():
        kb = jnp.concatenate([pg[0, 0].reshape(span, HEAD_A).astype(BF16) for pg in kpages], axis=0)
        vb = jnp.concatenate([pg[0, 0].reshape(span, HEAD_A).astype(BF16) for pg in vpages], axis=0)
        valid = jnp.concatenate([valid_of(mask_ref[0, :, u * PAGE:(u + 1) * PAGE], span)
                                 for u in range(ATT_PAGES)], axis=1)
        update(kb, vb, valid, False)

    @pl.when(step == pl.num_programs(1) - 1)
    def _():
        o_ref[0] = acc_ref[...] / l_ref[...]


def _attn_sample(page_table, layer, q, knew, vnew, mask, cache_k, cache_v):
    b, n_pages = page_table.shape
    rows = q.shape[1]
    nq = rows // N_HEADS_A
    past = n_pages * PAGE
    steps = n_pages // ATT_PAGES + 1
    pg = lambda s, u: jnp.maximum(s - 1, 0) * ATT_PAGES + u
    page_spec = lambda u: pl.BlockSpec((1, 1, PAGE, N_HEADS_A, HEAD_A),
                                       lambda bi, s, pt: (layer, pt[bi, pg(s, u)], 0, 0, 0))
    per_b = lambda shp: pl.BlockSpec((1,) + shp, lambda bi, s, pt: (bi, 0, 0))
    return pl.pallas_call(
        functools.partial(_attn_sample_kernel, nq=nq),
        out_shape=jax.ShapeDtypeStruct((b, rows, HEAD_A), F32),
        grid_spec=pltpu.PrefetchScalarGridSpec(
            num_scalar_prefetch=1,
            grid=(b, steps),
            in_specs=[per_b((rows, HEAD_A)), per_b((PAGE, HEAD_A)), per_b((PAGE, HEAD_A)),
                      pl.BlockSpec((1, MASK_ROWS, PAGE), lambda bi, s, pt: (bi, 0, past // PAGE)),
                      pl.BlockSpec((1, MASK_ROWS, ATT_PAGES * PAGE),
                                   lambda bi, s, pt: (bi, 0, jnp.maximum(s - 1, 0)))]
                     + [page_spec(u) for u in range(ATT_PAGES)] * 2,
            out_specs=per_b((rows, HEAD_A)),
            scratch_shapes=[pltpu.VMEM((rows, 1), F32), pltpu.VMEM((rows, 1), F32),
                            pltpu.VMEM((rows, HEAD_A), F32)],
        ),
        compiler_params=_cparams(("parallel", "arbitrary")),
        name="attn_sample",
    )(page_table, q, knew, vnew, mask, mask, *([cache_k] * ATT_PAGES), *([cache_v] * ATT_PAGES))


def _seg_sum(x, jmat):
    r = x.shape[0]
    xs = jnp.concatenate([x[:, p * LANES:(p + 1) * LANES] for p in range(N_PAIRS)], axis=0)
    hi, lo = _split_bf16(xs)
    ys = _dot(hi, jmat) + _dot(lo, jmat)
    return jnp.concatenate([ys[p * r:(p + 1) * r] for p in range(N_PAIRS)], axis=1)


def _rwkv_kernel(pb_ref, first_ref, s0_ref, mu_ref, w0_ref, a0_ref, kk_ref, ka_ref, rk_ref, lw_ref, lb_ref,
                 w2h, w2l, a2h, a2l, g2h, g2l, ob_ref, st_ref, carry_ref, state_ref, *, t_valid):
    nb, c = pb_ref.shape[0], pb_ref.shape[1]
    rows = nb * c
    n = pl.program_id(1)
    half = HEAD_B

    @pl.when(n == 0)
    def _():
        carry_ref[...] = first_ref[:, 0, :]
        z = jnp.zeros((half, half), F32)
        for b in range(nb):
            for p in range(N_PAIRS):
                top = jnp.concatenate([s0_ref[b, 2 * p], z], axis=1)
                bot = jnp.concatenate([z, s0_ref[b, 2 * p + 1]], axis=1)
                state_ref[b * N_PAIRS + p] = jnp.concatenate([top, bot], axis=0)

    pb = pb_ref[...].reshape(rows, PB_W)
    rowi = lax.broadcasted_iota(I32, (rows, 1), 0)
    step_i = rowi & (c - 1)
    carry = jnp.broadcast_to(carry_ref[...][:, None, :], (nb, c, PB_W)).reshape(rows, PB_W)
    prev = jnp.where(step_i == 0, carry, pltpu.roll(pb, 1, axis=0))
    carry_ref[...] = pb_ref[:, c - 1, :]
    xm = pb + (prev - pb) * mu_ref[...]

    r = xm[:, 0:C_B]
    k = xm[:, C_B:2 * C_B]
    v = xm[:, 2 * C_B:3 * C_B]
    xw = xm[:, 3 * C_B:3 * C_B + 128]
    xa = xm[:, 3 * C_B + 128:3 * C_B + 256]
    xg = xm[:, 3 * C_B + 256:3 * C_B + 512]
    w = -jax.nn.softplus(-(w0_ref[...] + _dot3w(jnp.tanh(xw), w2h, w2l))) - 0.5
    ld = -jnp.exp(w)
    a = jax.nn.sigmoid(a0_ref[...] + _dot3w(xa, a2h, a2l))
    g = _dot3w(jax.nn.sigmoid(xg), g2h, g2l)

    li = lax.broadcasted_iota(I32, (LANES, LANES), 0)
    lj = lax.broadcasted_iota(I32, (LANES, LANES), 1)
    same_head = (li // half) == (lj // half)
    jmat = jnp.where(same_head, 1.0, 0.0).astype(BF16)

    kkr = k * kk_ref[...]
    kk = kkr * lax.rsqrt(jnp.maximum(_seg_sum(kkr * kkr, jmat), 1e-24))
    k2 = k * (1.0 + (a - 1.0) * ka_ref[...])
    bonus = _seg_sum(r * k2 * rk_ref[...], jmat)
    avec = -kk
    bvec = kk * a
    if t_valid < c:
        live = step_i < t_valid
        ld = jnp.where(live, ld, 0.0)
        avec = jnp.where(live, avec, 0.0)
        bvec = jnp.where(live, bvec, 0.0)
        k2s = jnp.where(live, k2, 0.0)
        vs = jnp.where(live, v, 0.0)
    else:
        k2s, vs = k2, v

    ti = lax.broadcasted_iota(I32, (rows, rows), 0)
    tj = lax.broadcasted_iota(I32, (rows, rows), 1)
    tri = jnp.where(jnp.logical_and(tj <= ti, (ti // c) == (tj // c)), 1.0, 0.0).astype(BF16)
    l1 = ld.astype(BF16)
    rem = ld - l1.astype(F32)
    l2 = rem.astype(BF16)
    l3 = (rem - l2.astype(F32)).astype(BF16)
    logc = _dot(tri, l1) + _dot(tri, l2) + _dot(tri, l3)
    logc_end = jnp.broadcast_to(logc.reshape(nb, c, C_B)[:, c - 1:c, :], (nb, c, C_B)).reshape(rows, C_B)
    e_pos = jnp.exp(logc)
    e_neg = jnp.exp(-logc)
    e_end = jnp.exp(logc_end - logc)

    def grp(x):
        return jnp.stack([x[b * c:(b + 1) * c, p * LANES:(p + 1) * LANES]
                          for b in range(nb) for p in range(N_PAIRS)], axis=0)

    rt = grp(r * e_pos)
    at = grp(avec * jnp.exp(logc - ld))
    kt = grp(k2s * e_neg)
    bt = grp(bvec * e_neg)
    kc = grp(k2s * e_end)
    bc = grp(bvec * e_end)
    vg = grp(vs)
    dec_end = grp(jnp.exp(logc_end))[:, 0:1, :]

    lane = lax.broadcasted_iota(I32, (1, 1, LANES), 2)
    m0 = lane < half
    col4 = lax.broadcasted_iota(I32, (1, c, 4 * c), 2) & (c - 1)
    row4 = lax.broadcasted_iota(I32, (1, c, 4 * c), 1)
    eye2 = jnp.where(lax.broadcasted_iota(I32, (1, 2 * c, 2 * c), 1)
                     == lax.broadcasted_iota(I32, (1, 2 * c, 2 * c), 2), 1.0, 0.0)
    lane2c = lax.broadcasted_iota(I32, (1, 1, 2 * c), 2) < c

    def two(x):
        return jnp.concatenate([jnp.where(m0, x, 0.0), jnp.where(m0, 0.0, x)], axis=1)

    s_old = state_ref[...]
    lhs1 = jnp.concatenate([rt, at], axis=1).astype(BF16)
    rhs1 = jnp.concatenate([two(kt), two(bt)], axis=1).astype(BF16)
    amat = _bmm_nt(lhs1, rhs1)
    a_r = jnp.where(col4 <= row4, amat[:, :c], 0.0)
    a_a = jnp.where(col4 < row4, amat[:, c:], 0.0)
    x0 = _bmm_nt(lhs1, s_old.astype(BF16))
    vm = two(vg).astype(BF16)
    x = x0[:, c:] + _bmm(a_a[:, :, :2 * c].astype(BF16), vm)
    lab = a_a[:, :, 2 * c:]
    lbd = jnp.concatenate([jnp.where(lane2c, lab, 0.0), jnp.where(lane2c, 0.0, lab)], axis=1)
    pm = eye2 + lbd
    lk = lbd
    for _ in range(int(np.log2(c)) - 1):
        lkb = lk.astype(BF16)
        lk = _bmm(lkb, lkb)
        pm = pm + _bmm(pm.astype(BF16), lk.astype(BF16))
    us = _bmm(pm.astype(BF16), two(x).astype(BF16))
    u = us[:, :c] + us[:, c:]
    y = x0[:, :c] + _bmm(a_r.astype(BF16), jnp.concatenate([vm, two(u).astype(BF16)], axis=1))
    vals = jnp.concatenate([vg, u], axis=1)
    keys = jnp.concatenate([kc, bc], axis=1).astype(BF16)
    add = _bmm(jnp.swapaxes(vals, 1, 2).astype(BF16), keys)
    state_ref[...] = s_old * dec_end + jnp.where(same_head[None], add, 0.0)

    y = jnp.concatenate([jnp.concatenate([y[b * N_PAIRS + p] for p in range(N_PAIRS)], axis=1)
                         for b in range(nb)], axis=0)
    mean = _seg_sum(y, jmat) * (1.0 / half)
    d = y - mean
    var = _seg_sum(d * d, jmat) * (1.0 / half)
    yn = d * lax.rsqrt(var + LNX_EPS) * lw_ref[...] + lb_ref[...]
    ob_ref[...] = ((yn + bonus * v) * g).astype(ob_ref.dtype).reshape(nb, c, C_B)

    @pl.when(n == pl.num_programs(1) - 1)
    def _():
        for b in range(nb):
            for p in range(N_PAIRS):
                s_new = state_ref[b * N_PAIRS + p]
                st_ref[b, 2 * p] = s_new[:half, :half]
                st_ref[b, 2 * p + 1] = s_new[half:, half:]


def _rwkv(pb, first, s0, vecs, loras, t_valid):
    b, t, _ = pb.shape
    nb = int(np.gcd(b, RWKV_NB))
    vec = lambda a: pl.BlockSpec(a.shape, lambda bi, n: (0, 0))
    st_spec = pl.BlockSpec((nb,) + s0.shape[1:], lambda bi, n: (bi, 0, 0, 0))
    return pl.pallas_call(
        functools.partial(_rwkv_kernel, t_valid=t_valid),
        out_shape=(jax.ShapeDtypeStruct((b, t, C_B), BF16), jax.ShapeDtypeStruct(s0.shape, F32)),
        grid=(b // nb, t // CHUNK),
        in_specs=[pl.BlockSpec((nb, CHUNK, PB_W), lambda bi, n: (bi, n, 0)),
                  pl.BlockSpec((nb, 1, PB_W), lambda bi, n: (bi, 0, 0)), st_spec]
                 + [vec(a) for a in vecs] + [vec(a) for a in loras],
        out_specs=(pl.BlockSpec((nb, CHUNK, C_B), lambda bi, n: (bi, n, 0)), st_spec),
        scratch_shapes=[pltpu.VMEM((nb, PB_W), F32), pltpu.VMEM((nb * N_PAIRS, LANES, LANES), F32)],
        compiler_params=_cparams(("parallel", "arbitrary")),
        name="rwkv7",
    )(pb, first, s0, *vecs, *loras)


def _merge_kernel(oa_ref, ob_ref, wpa_ref, wpb_ref, ga_ref, gb_ref, o_ref):
    za = _dot(oa_ref[...], wpa_ref[...])
    zb = _dot(ob_ref[...], wpb_ref[...])
    o_ref[...] = (ga_ref[...].astype(F32) * za + gb_ref[...].astype(F32) * zb).astype(o_ref.dtype)


def _merge(oa, ob, wpa, wpb, gates, tn=1024):
    m = oa.shape[0]
    tm = _row_tile(m, 1024)
    nb = D_MODEL // tn
    return pl.pallas_call(
        _merge_kernel,
        out_shape=jax.ShapeDtypeStruct((m, D_MODEL), BF16),
        grid=(m // tm, nb),
        in_specs=[pl.BlockSpec((tm, C_A), lambda i, j: (i, 0)), pl.BlockSpec((tm, C_B), lambda i, j: (i, 0)),
                  pl.BlockSpec((C_A, tn), lambda i, j: (0, j)), pl.BlockSpec((C_B, tn), lambda i, j: (0, j)),
                  pl.BlockSpec((tm, tn), lambda i, j: (i, j)), pl.BlockSpec((tm, tn), lambda i, j: (i, j + nb))],
        out_specs=pl.BlockSpec((tm, tn), lambda i, j: (i, j)),
        compiler_params=_cparams(("parallel", "parallel")),
        name="merge",
    )(oa, ob, wpa, wpb, gates, gates)


def _oproj_kernel(x_ref, mix_ref, wo_ref, g_ref, x1_ref, h2_ref):
    x1 = x_ref[...] + _dot(mix_ref[...], wo_ref[...])
    x1_ref[...] = x1
    h2 = x1 * lax.rsqrt(jnp.mean(x1 * x1, axis=-1, keepdims=True) + NORM_EPS) * g_ref[...]
    h2_ref[...] = h2.astype(BF16)


def _oproj(x, mixed, wo, g):
    m, d = x.shape
    tm = _row_tile(m)
    row = pl.BlockSpec((tm, d), lambda i: (i, 0))
    return pl.pallas_call(
        _oproj_kernel,
        out_shape=(jax.ShapeDtypeStruct((m, d), F32), jax.ShapeDtypeStruct((m, d), BF16)),
        grid=(m // tm,),
        in_specs=[row, row, pl.BlockSpec((d, d), lambda i: (0, 0)), pl.BlockSpec((1, d), lambda i: (0, 0))],
        out_specs=(row, row),
        compiler_params=_cparams(("parallel",)),
        name="oproj_norm",
    )(x, mixed, wo, g.reshape(1, d))


def _ffn_up_kernel(h_ref, wg_ref, wu_ref, o_ref, wg_bf, wu_bf):
    @pl.when(pl.program_id(1) == 0)
    def _():
        wg_bf[...] = wg_ref[...].astype(BF16)
        wu_bf[...] = wu_ref[...].astype(BF16)

    h = h_ref[...]
    gate = _dot(h, wg_bf[...])
    up = _dot(h, wu_bf[...])
    o_ref[...] = (gate * jax.nn.sigmoid(gate) * up).astype(o_ref.dtype)


def _ffn_up(h2, wg, wu, tn=512):
    m, d = h2.shape
    n = wg.shape[1]
    tm = _row_tile(m, 1024)
    return pl.pallas_call(
        _ffn_up_kernel,
        out_shape=jax.ShapeDtypeStruct((m, n), BF16),
        grid=(n // tn, m // tm),
        in_specs=[pl.BlockSpec((tm, d), lambda j, i: (i, 0)), pl.BlockSpec((d, tn), lambda j, i: (0, j)),
                  pl.BlockSpec((d, tn), lambda j, i: (0, j))],
        out_specs=pl.BlockSpec((tm, tn), lambda j, i: (i, j)),
        scratch_shapes=[pltpu.VMEM((d, tn), BF16), pltpu.VMEM((d, tn), BF16)],
        compiler_params=_cparams(("arbitrary", "arbitrary")),
        name="ffn_up",
    )(h2, wg, wu)


def _ffn_down_kernel(x_ref, a_ref, wd_ref, o_ref, wd_bf):
    @pl.when(pl.program_id(1) == 0)
    def _():
        wd_bf[...] = wd_ref[...].astype(BF16)

    o_ref[...] = x_ref[...] + _dot(a_ref[...], wd_bf[...])


def _ffn_down(x1, act, wd, tn=512):
    m, d = x1.shape
    kdim = act.shape[1]
    tm = _row_tile(m)
    return pl.pallas_call(
        _ffn_down_kernel,
        out_shape=jax.ShapeDtypeStruct((m, d), F32),
        grid=(d // tn, m // tm),
        in_specs=[pl.BlockSpec((tm, tn), lambda j, i: (i, j)), pl.BlockSpec((tm, kdim), lambda j, i: (i, 0)),
                  pl.BlockSpec((kdim, tn), lambda j, i: (0, j))],
        out_specs=pl.BlockSpec((tm, tn), lambda j, i: (i, j)),
        scratch_shapes=[pltpu.VMEM((kdim, tn), BF16)],
        compiler_params=_cparams(("arbitrary", "arbitrary")),
        name="ffn_down",
    )(x1, act, wd)


def _pad_cols(w, width):
    return jnp.pad(w, ((0, 0), (0, width - w.shape[1])))


def _pad_rows(w, height):
    return jnp.pad(w, ((0, height - w.shape[0]), (0, 0)))


def _prep_weights(w_in, mu_b, w2, a2, g2):
    o = 0
    cols = {}
    for name, wdt in (("q", C_A), ("k", C_A), ("v", C_A), ("qi", IDX_HEADS * IDX_DIM), ("ki", IDX_DIM),
                      ("wi", IDX_HEADS), ("g", 2 * D_MODEL), ("rkv", 3 * C_B), ("xw", D_DECAY), ("xa", D_AAA),
                      ("xg", D_GATE)):
        cols[name] = (o, o + wdt)
        o += wdt
    cut = lambda name: w_in[:, cols[name][0]:cols[name][1]]
    w_idx = _pad_cols(jnp.concatenate([cut("qi"), cut("ki"), cut("wi")], axis=1), IDX_W)
    idx_hi, idx_lo = _split_bf16(w_idx)
    w_b = jnp.concatenate([cut("rkv"), _pad_cols(cut("xw"), 128), _pad_cols(cut("xa"), 128),
                           _pad_cols(cut("xg"), 256)], axis=1)
    nb0 = cols["rkv"][0]
    mu = lambda name: mu_b[cols[name][0] - nb0:cols[name][1] - nb0][None, :]
    mu_p = jnp.concatenate([mu("rkv"), _pad_cols(mu("xw"), 128), _pad_cols(mu("xa"), 128),
                            _pad_cols(mu("xg"), 256)], axis=1)
    loras = []
    for wl, hgt in ((w2, 128), (a2, 128), (g2, 256)):
        loras += list(_split_bf16(_pad_rows(wl, hgt)))
    return dict(q=cut("q").astype(BF16), k=cut("k").astype(BF16), v=cut("v").astype(BF16),
                idx_hi=idx_hi, idx_lo=idx_lo, g=cut("g").astype(BF16), b=w_b.astype(BF16), mu=mu_p,
                loras=loras)


def _layer(x, pos, wts, p, *, seq, attn_fn, first_pb, s0, t_valid):
    m = x.shape[0]
    nb = m // seq
    hh, hl = _norm_split(x, p["g_mix"])
    tabs_a = _rope_tables(pos, HEAD_A)
    tabs_i = _rope_tables(pos, IDX_DIM)
    (q_b,) = _proj(hh, wts["q"], tabs_a, rope=True, want_f32=False, want_bf16=True, scale=HEAD_A ** -0.5)
    k_f, k_b = _proj(hh, wts["k"], tabs_a, rope=True, want_f32=True, want_bf16=True)
    v_f, v_b = _proj(hh, wts["v"], tabs_a, rope=False, want_f32=True, want_bf16=True)
    qcat, kcat, ki_f, wi = _idx_proj(hh, hl, wts["idx_hi"], wts["idx_lo"], tabs_i)
    gates = _mm(hh, wts["g"], tn=1024, out_dtype=BF16, act="sigmoid")
    pb = _mm(hh, wts["b"], tn=PB_W // 4, out_dtype=F32)

    o_a = attn_fn(q_b, k_b, v_b, qcat, kcat, ki_f, wi)

    t_pad = -(-seq // CHUNK) * CHUNK
    pb3 = pb.reshape(nb, seq, PB_W)
    if t_pad != seq:
        pb3 = jnp.pad(pb3, ((0, 0), (0, t_pad - seq), (0, 0)))
    vecs = [wts["mu"]] + [p[n].reshape(1, C_B) for n in ("w0", "a0", "k_k", "k_a", "r_k", "lnx_w", "lnx_b")]
    o_b, wkv = _rwkv(pb3, first_pb, s0, vecs, wts["loras"], t_valid)
    o_b = o_b[:, :seq].reshape(m, C_B)

    mixed = _merge(o_a, o_b, p["w_pa"], p["w_pb"], gates)
    x1, h2 = _oproj(x, mixed, p["w_o"], p["g_ffn"])
    act = _ffn_up(h2, p["w_gate"], p["w_up"])
    x2 = _ffn_down(x1, act, p["w_down"])
    shift = _norm_f32(x.reshape(nb, seq, D_MODEL)[:, -1], p["g_mix"])
    return x2, k_f, v_f, ki_f, wkv, shift


def kernel(x_prompt, x_sample, cache_k, cache_v, cache_kidx, state_wkv, state_shift, page_table, g_mix, w_in, mu_b, w0, w2, a0, a2, g2, k_k, k_a, r_k, lnx_w, lnx_b, w_pa, w_pb, w_o, g_ffn, w_gate, w_up, w_down, g_final):
    bp, sp, d = x_prompt.shape
    bs, ts, _ = x_sample.shape
    depth = w_in.shape[0]
    n_pages = page_table.shape[1]
    past = n_pages * PAGE
    xp = x_prompt.reshape(bp * sp, d)
    xs = x_sample.reshape(bs * ts, d)
    outs_p, outs_s = [], []
    for l in range(depth):
        wts = _prep_weights(w_in[l], mu_b[l], w2[l], a2[l], g2[l])
        p = dict(g_mix=g_mix[l], w0=w0[l], a0=a0[l], k_k=k_k[l], k_a=k_a[l], r_k=r_k[l], lnx_w=lnx_w[l],
                 lnx_b=lnx_b[l], w_pa=w_pa[l].astype(BF16), w_pb=w_pb[l].astype(BF16), w_o=w_o[l].astype(BF16),
                 g_ffn=g_ffn[l], w_gate=w_gate[l], w_up=w_up[l], w_down=w_down[l])

        def attn_p(q_b, k_b, v_b, qcat, kcat, ki_f, wi):
            r3 = lambda a: a.reshape(bp, sp, a.shape[-1])
            return _attn_prompt(r3(q_b), r3(k_b), r3(v_b), r3(qcat), r3(kcat), r3(wi)).reshape(bp * sp, C_A)

        def attn_s(q_b, k_b, v_b, qcat, kcat, ki_f, wi):
            qc = qcat.reshape(bs, ts * IDX_HEADS, 256)
            qh, ql = qc[..., :IDX_DIM], qc[..., IDX_DIM:2 * IDX_DIM]
            wrow = wi.reshape(bs, ts * IDX_HEADS, 1)
            ki_new = jnp.pad(ki_f.reshape(bs, ts, IDX_DIM), ((0, 0), (0, PAGE - ts), (0, 0)))
            mask = _sel_sample(page_table, l, qh, ql, wrow, ki_new, cache_kidx, ts)
            as_rows = lambda a: a.reshape(bs, ts * N_HEADS_A, HEAD_A)
            pad_rows = lambda a: jnp.pad(as_rows(a), ((0, 0), (0, PAGE - ts * N_HEADS_A), (0, 0)))
            o = _attn_sample(page_table, l, as_rows(q_b), pad_rows(k_b), pad_rows(v_b), mask, cache_k, cache_v)
            return o.reshape(bs * ts, C_A).astype(BF16)

        zero_first = jnp.zeros((bp, 1, PB_W), F32)
        zero_state = jnp.zeros((bp, N_HEADS_B, HEAD_B, HEAD_B), F32)
        xp, kp, vp, kip, wkvp, shp = _layer(xp, jnp.arange(sp), wts, p, seq=sp, attn_fn=attn_p,
                                            first_pb=zero_first, s0=zero_state, t_valid=CHUNK)
        first_s = _mm(state_shift[l].astype(BF16), wts["b"], tn=PB_W // 4, out_dtype=F32).reshape(bs, 1, PB_W)
        pos_s = jnp.tile(past + jnp.arange(ts), bs)
        xs, ksm, vsm, kis, wkvs, shs = _layer(xs, pos_s, wts, p, seq=ts, attn_fn=attn_s,
                                              first_pb=first_s, s0=state_wkv[l], t_valid=ts)
        outs_p.append((kp.reshape(bp, sp, N_HEADS_A, HEAD_A), vp.reshape(bp, sp, N_HEADS_A, HEAD_A),
                       kip.reshape(bp, sp, IDX_DIM), wkvp, shp))
        outs_s.append((ksm.reshape(bs, ts, N_HEADS_A, HEAD_A), vsm.reshape(bs, ts, N_HEADS_A, HEAD_A),
                       kis.reshape(bs, ts, IDX_DIM), wkvs, shs))
    y_prompt = _norm_f32(xp, g_final).reshape(bp, sp, d)
    y_sample = _norm_f32(xs, g_final).reshape(bs, ts, d)
    stack = lambda outs, i: jnp.stack([o[i] for o in outs])
    return ((y_prompt, y_sample) + tuple(stack(outs_p, i) for i in range(5))
            + tuple(stack(outs_s, i) for i in range(5)))
```

```python
import functools

import numpy as np
import jax
import jax.numpy as jnp
from jax import lax
from jax.experimental import pallas as pl
from jax.experimental.pallas import tpu as pltpu

F32 = jnp.float32
BF16 = jnp.bfloat16
I32 = jnp.int32
I16 = jnp.int16

D_MODEL = 2048
HEAD_A = 128
N_HEADS_A = 8
C_A = N_HEADS_A * HEAD_A
IDX_HEADS = 8
IDX_DIM = 64
TOPK = 256
HEAD_B = 64
N_HEADS_B = 16
C_B = N_HEADS_B * HEAD_B
D_DECAY = 64
D_AAA = 64
D_GATE = 160
LNX_EPS = 64e-5
ROPE_THETA = 500000.0
NORM_EPS = 1e-6
PAGE = 128
NEG = -1e30
INT_MIN = -(2 ** 31)

LANES = 128
VMEM_LIMIT = 56 * 1024 * 1024
CHUNK = 64
RWKV_NB = 4
N_PAIRS = N_HEADS_B // 2
TQ = 256
RANK_BLK = 256
MASK_ROWS = 16
PB_W = 3 * C_B + 128 + 128 + 256
IDX_W = 640


def _cparams(sem):
    return pltpu.CompilerParams(dimension_semantics=sem, vmem_limit_bytes=VMEM_LIMIT)


def _split_bf16(x):
    hi = x.astype(BF16)
    lo = (x - hi.astype(F32)).astype(BF16)
    return hi, lo


def _dot(a, b):
    return jnp.dot(a, b, preferred_element_type=F32)


def _dot_nt(a, b):
    return lax.dot_general(a, b, (((1,), (1,)), ((), ())), preferred_element_type=F32)


def _bmm(a, b):
    return lax.dot_general(a, b, (((2,), (1,)), ((0,), (0,))), preferred_element_type=F32)


def _bmm_nt(a, b):
    return lax.dot_general(a, b, (((2,), (2,)), ((0,), (0,))), preferred_element_type=F32)


def _dot3w(x, wh_ref, wl_ref):
    xh, xl = _split_bf16(x)
    wh = wh_ref[...]
    return _dot(xh, wh) + _dot(xl, wh) + _dot(xh, wl_ref[...])


def _norm_split_kernel(x_ref, g_ref, hi_ref, lo_ref):
    x = x_ref[...]
    y = x * lax.rsqrt(jnp.mean(x * x, axis=-1, keepdims=True) + NORM_EPS) * g_ref[...]
    hi = y.astype(BF16)
    hi_ref[...] = hi
    lo_ref[...] = (y - hi.astype(F32)).astype(BF16)


def _norm_f32_kernel(x_ref, g_ref, o_ref):
    x = x_ref[...]
    o_ref[...] = x * lax.rsqrt(jnp.mean(x * x, axis=-1, keepdims=True) + NORM_EPS) * g_ref[...]


def _row_tile(m, pref=512):
    for tm in (pref, 512):
        if m % tm == 0:
            return tm
    return m


def _norm_split(x, g):
    m, d = x.shape
    tm = _row_tile(m)
    return pl.pallas_call(
        _norm_split_kernel,
        out_shape=(jax.ShapeDtypeStruct((m, d), BF16), jax.ShapeDtypeStruct((m, d), BF16)),
        grid=(m // tm,),
        in_specs=[pl.BlockSpec((tm, d), lambda i: (i, 0)), pl.BlockSpec((1, d), lambda i: (0, 0))],
        out_specs=(pl.BlockSpec((tm, d), lambda i: (i, 0)), pl.BlockSpec((tm, d), lambda i: (i, 0))),
        compiler_params=_cparams(("parallel",)),
        name="norm_split",
    )(x, g.reshape(1, d))


def _norm_f32(x, g):
    m, d = x.shape
    tm = _row_tile(m)
    return pl.pallas_call(
        _norm_f32_kernel,
        out_shape=jax.ShapeDtypeStruct((m, d), F32),
        grid=(m // tm,),
        in_specs=[pl.BlockSpec((tm, d), lambda i: (i, 0)), pl.BlockSpec((1, d), lambda i: (0, 0))],
        out_specs=pl.BlockSpec((tm, d), lambda i: (i, 0)),
        compiler_params=_cparams(("parallel",)),
        name="norm_f32",
    )(x, g.reshape(1, d))


def _rope_tables(pos, head_dim):
    rot = head_dim // 4
    half = rot // 2
    inv = ROPE_THETA ** (-jnp.arange(half, dtype=F32) * 2.0 / rot)
    ang = pos.astype(F32)[:, None] * inv[None, :]
    cos, sin = jnp.cos(ang), jnp.sin(ang)
    t = pos.shape[0]
    c = jnp.ones((t, head_dim), F32).at[:, :half].set(cos).at[:, half:rot].set(cos)
    s1 = jnp.zeros((t, head_dim), F32).at[:, half:rot].set(sin)
    s2 = jnp.zeros((t, head_dim), F32).at[:, :half].set(-sin)
    rep = LANES // head_dim
    return tuple(jnp.tile(a, (1, rep)) for a in (c, s1, s2))


def _rope_lanes(x, c, s1, s2, half):
    return x * c + pltpu.roll(x, half, axis=1) * s1 + pltpu.roll(x, LANES - half, axis=1) * s2


def _proj_kernel(a_ref, w_ref, c_ref, s1_ref, s2_ref, *out_refs, rope, want_f32, want_bf16, scale):
    acc = _dot(a_ref[...], w_ref[...])
    if rope:
        c, s1, s2 = c_ref[...], s1_ref[...], s2_ref[...]
        parts = [_rope_lanes(acc[:, h * LANES:(h + 1) * LANES], c, s1, s2, HEAD_A // 8)
                 for h in range(acc.shape[1] // LANES)]
        acc = jnp.concatenate(parts, axis=1)
    if scale is not None:
        acc = acc * scale
    k = 0
    if want_f32:
        out_refs[k][...] = acc
        k += 1
    if want_bf16:
        out_refs[k][...] = acc.astype(BF16)


def _proj(a, w, tabs, *, rope, want_f32, want_bf16, scale=None):
    m, kdim = a.shape
    n = w.shape[1]
    tm = _row_tile(m)
    ntab = tabs[0].shape[0] // tm
    outs, ospecs = [], []
    for want, dt in ((want_f32, F32), (want_bf16, BF16)):
        if want:
            outs.append(jax.ShapeDtypeStruct((m, n), dt))
            ospecs.append(pl.BlockSpec((tm, n), lambda i: (i, 0)))
    tab_spec = pl.BlockSpec((tm, LANES), lambda i: (i % ntab, 0))
    res = pl.pallas_call(
        functools.partial(_proj_kernel, rope=rope, want_f32=want_f32, want_bf16=want_bf16, scale=scale),
        out_shape=tuple(outs),
        grid=(m // tm,),
        in_specs=[pl.BlockSpec((tm, kdim), lambda i: (i, 0)), pl.BlockSpec((kdim, n), lambda i: (0, 0)),
                  tab_spec, tab_spec, tab_spec],
        out_specs=tuple(ospecs),
        compiler_params=_cparams(("parallel",)),
        name="proj_rope" if rope else "proj",
    )(a, w, *tabs)
    return res


def _mm_kernel(a_ref, w_ref, o_ref, *, act):
    acc = _dot(a_ref[...], w_ref[...])
    if act == "sigmoid":
        acc = jax.nn.sigmoid(acc)
    o_ref[...] = acc.astype(o_ref.dtype)


def _mm(a, w, *, tn, out_dtype, act=None):
    m, kdim = a.shape
    n = w.shape[1]
    tm = _row_tile(m, 1024)
    return pl.pallas_call(
        functools.partial(_mm_kernel, act=act),
        out_shape=jax.ShapeDtypeStruct((m, n), out_dtype),
        grid=(m // tm, n // tn),
        in_specs=[pl.BlockSpec((tm, kdim), lambda i, j: (i, 0)), pl.BlockSpec((kdim, tn), lambda i, j: (0, j))],
        out_specs=pl.BlockSpec((tm, tn), lambda i, j: (i, j)),
        compiler_params=_cparams(("parallel", "parallel")),
        name="mm_" + (act or "plain"),
    )(a, w)


def _idx_proj_kernel(hh_ref, hl_ref, wh_ref, wl_ref, c_ref, s1_ref, s2_ref,
                     qcat_ref, kcat_ref, kf_ref, wi_ref):
    hh = hh_ref[...]
    wh = wh_ref[...]
    acc = _dot(hh, wh) + _dot(hl_ref[...], wh) + _dot(hh, wl_ref[...])
    c, s1, s2 = c_ref[...], s1_ref[...], s2_ref[...]
    half = IDX_DIM // 8
    zeros = jnp.zeros((acc.shape[0], IDX_DIM), F32)

    def hi_lo(x):
        hi = x.astype(BF16).astype(F32)
        return hi, (x - hi).astype(BF16).astype(F32)

    pieces = []
    for g in range(IDX_HEADS * IDX_DIM // LANES):
        blk = _rope_lanes(acc[:, g * LANES:(g + 1) * LANES], c, s1, s2, half)
        for u in range(LANES // IDX_DIM):
            hi, lo = hi_lo(blk[:, u * IDX_DIM:(u + 1) * IDX_DIM])
            pieces += [hi, lo, hi, zeros]
    qcat_ref[...] = jnp.concatenate(pieces, axis=1).astype(BF16)
    last = acc[:, IDX_HEADS * IDX_DIM:]
    ki = _rope_lanes(last, c, s1, s2, half)[:, :IDX_DIM]
    kf_ref[...] = ki
    hi, lo = hi_lo(ki)
    kcat_ref[...] = jnp.concatenate([hi, hi, lo, zeros], axis=1).astype(BF16)
    wi_ref[...] = last[:, IDX_DIM:IDX_DIM + IDX_HEADS] * (IDX_HEADS ** -0.5 * IDX_DIM ** -0.5)


def _idx_proj(hh, hl, wh, wl, tabs):
    m, d = hh.shape
    tm = _row_tile(m)
    ntab = tabs[0].shape[0] // tm
    row = lambda w: pl.BlockSpec((tm, w), lambda i: (i, 0))
    full = lambda a: pl.BlockSpec(a.shape, lambda i: (0, 0))
    tab_spec = pl.BlockSpec((tm, LANES), lambda i: (i % ntab, 0))
    return pl.pallas_call(
        _idx_proj_kernel,
        out_shape=(jax.ShapeDtypeStruct((m, IDX_HEADS * 256), BF16), jax.ShapeDtypeStruct((m, 256), BF16),
                   jax.ShapeDtypeStruct((m, IDX_DIM), F32), jax.ShapeDtypeStruct((m, IDX_HEADS), F32)),
        grid=(m // tm,),
        in_specs=[row(d), row(d), full(wh), full(wl), tab_spec, tab_spec, tab_spec],
        out_specs=(row(IDX_HEADS * 256), row(256), row(IDX_DIM), row(IDX_HEADS)),
        compiler_params=_cparams(("parallel",)),
        name="idx_proj",
    )(hh, hl, wh, wl, *tabs)


def _sort_key(s):
    bits = lax.bitcast_convert_type(s, I32)
    return bits ^ (lax.shift_right_arithmetic(bits, 31) & 0x7FFFFFFF)


def _kth_largest_16(vals, need):
    r, l = vals.shape

    def count_ge(cand):
        one = jnp.where(vals >= cand, jnp.int16(1), jnp.int16(0))
        acc = one[:, 0:LANES]
        for b in range(1, l // LANES):
            acc = acc + one[:, b * LANES:(b + 1) * LANES]
        return jnp.sum(acc.astype(F32), axis=1, keepdims=True)

    def body(it, prefix):
        cand_u = prefix | lax.shift_left(jnp.int32(1), 15 - it)
        cand = (cand_u - 32768).astype(I16)
        return jnp.where(count_ge(cand) >= need, cand_u, prefix)

    return lax.fori_loop(0, 16, body, jnp.zeros((r, 1), I32), unroll=True) - 32768


def _kth_largest_key(key, n_sel):
    hi = lax.shift_right_arithmetic(key, 16)
    lo = (key & 0xFFFF) - 32768
    t_hi = _kth_largest_16(hi.astype(I16), jnp.float32(n_sel))
    n_above = jnp.sum(jnp.where(hi > t_hi, 1.0, 0.0), axis=1, keepdims=True)
    lo_in = jnp.where(hi == t_hi, lo, -32768).astype(I16)
    t_lo = _kth_largest_16(lo_in, n_sel - n_above)
    return lax.shift_left(t_hi, 16) | (t_lo + 32768)


def _tie_select(key, thr, n_sel, blk):
    r, l = key.shape
    gt = key > thr
    eq = key == thr
    need = n_sel - jnp.sum(jnp.where(gt, 1.0, 0.0), axis=1, keepdims=True)
    rr = lax.broadcasted_iota(I32, (blk, blk), 0)
    cc = lax.broadcasted_iota(I32, (blk, blk), 1)
    upper = jnp.where(rr < cc, 1.0, 0.0).astype(BF16)
    offs = jnp.zeros((r, 1), F32)
    parts = []
    for b in range(l // blk):
        e = jnp.where(eq[:, b * blk:(b + 1) * blk], 1.0, 0.0)
        rank = _dot(e.astype(BF16), upper) + offs
        parts.append(jnp.logical_and(eq[:, b * blk:(b + 1) * blk], rank < need))
        offs = offs + jnp.sum(e, axis=1, keepdims=True)
    return jnp.logical_or(gt, jnp.concatenate(parts, axis=1))


def _selection_bias(s, causal, n_sel, blk, bias_ref):
    key = _sort_key(s)
    thr = _kth_largest_key(key, n_sel)
    ge_c = jnp.logical_and(key >= thr, causal)
    cnt = (jnp.sum(jnp.where(ge_c, 1.0, 0.0), axis=1, keepdims=True)
           + jnp.sum(jnp.where(jnp.logical_or(key <= thr, causal), 0.0, 1.0), axis=1, keepdims=True))
    bias_ref[...] = jnp.where(ge_c, 0.0, NEG)

    @pl.when(jnp.max(cnt) > n_sel)
    def _():
        sel = _tie_select(key, thr, n_sel, blk)
        bias_ref[...] = jnp.where(jnp.logical_and(sel, causal), 0.0, NEG)


def _attn_prompt_kernel(q_ref, k_ref, v_ref, qi_ref, ki_ref, wi_ref, o_ref, bias_ref, *, n_sel, q0):
    tq = q_ref.shape[1]
    width = k_ref.shape[1]
    ki = ki_ref[0]
    wi = wi_ref[0]
    score = jnp.zeros((tq, width), F32)
    for h in range(IDX_HEADS):
        lg = _dot_nt(qi_ref[0, :, h * 256:(h + 1) * 256], ki)
        score = score + wi[:, h:h + 1] * jnp.maximum(lg, 0.0)
    col = lax.broadcasted_iota(I32, (tq, width), 1)
    row = q0 + lax.broadcasted_iota(I32, (tq, width), 0)
    causal = col <= row
    _selection_bias(jnp.where(causal, score, NEG), causal, n_sel, RANK_BLK, bias_ref)
    bias = bias_ref[...]
    for h in range(N_HEADS_A):
        sl = slice(h * HEAD_A, (h + 1) * HEAD_A)
        s = _dot_nt(q_ref[0, :, sl], k_ref[0, :, sl]) + bias
        m = jnp.max(s, axis=1, keepdims=True)
        p = jnp.exp(s - m)
        l = jnp.sum(p, axis=1, keepdims=True)
        o = _dot(p.astype(BF16), v_ref[0, :, sl]) / l
        o_ref[0, :, sl] = o.astype(o_ref.dtype)


def _attn_prompt(q, k, v, qcat, kcat, wi):
    b, s_len, _ = q.shape
    tq = min(TQ, s_len)
    n_sel = min(TOPK, s_len // 4)
    outs = []
    for i in range(s_len // tq):
        width = (i + 1) * tq
        blk_q = lambda w, i=i: pl.BlockSpec((1, tq, w), lambda bi: (bi, i, 0))
        blk_k = lambda w, width=width: pl.BlockSpec((1, width, w), lambda bi: (bi, 0, 0))
        outs.append(pl.pallas_call(
            functools.partial(_attn_prompt_kernel, n_sel=n_sel, q0=i * tq),
            out_shape=jax.ShapeDtypeStruct((b, tq, C_A), BF16),
            grid=(b,),
            in_specs=[blk_q(C_A), blk_k(C_A), blk_k(C_A), blk_q(IDX_HEADS * 256), blk_k(256), blk_q(IDX_HEADS)],
            out_specs=pl.BlockSpec((1, tq, C_A), lambda bi: (bi, 0, 0)),
            scratch_shapes=[pltpu.VMEM((tq, width), F32)],
            compiler_params=_cparams(("parallel",)),
            name="attn_prompt_%d" % i,
        )(q, k, v, qcat, kcat, wi))
    return jnp.concatenate(outs, axis=1)


SEL_PAGES = 16
ATT_PAGES = 8


def _score_sample_kernel(pt_ref, qh_ref, ql_ref, w_ref, knew_ref, *rest, past):
    pages = rest[:SEL_PAGES]
    sc_ref = rest[SEL_PAGES]
    step = pl.program_id(1)
    qh, ql, w = qh_ref[0], ql_ref[0], w_ref[0]
    nq = qh.shape[0] // IDX_HEADS

    def scores_of(kt):
        kh, kl = _split_bf16(kt)
        lg = _dot(qh, kh) + _dot(ql, kh) + _dot(qh, kl)
        x = w * jnp.maximum(lg, 0.0)
        return jnp.sum(x.reshape(nq, IDX_HEADS, kt.shape[1]), axis=1)

    @pl.when(step == 0)
    def _():
        sc_ref[0] = jnp.zeros(sc_ref.shape[1:], F32)
        sc_ref[0, 0:nq, past:past + PAGE] = scores_of(knew_ref[0])

    span = SEL_PAGES * PAGE
    off = pl.multiple_of(step * span, span)
    sc_ref[0, 0:nq, pl.ds(off, span)] = scores_of(jnp.concatenate([pg[0, 0] for pg in pages], axis=1))


def _select_sample_kernel(s_ref, mask_ref, bias_ref, *, n_sel, past, t_new, nq):
    s = s_ref[...]
    col = lax.broadcasted_iota(I32, s.shape, 1)
    qpos = past + (lax.broadcasted_iota(I32, s.shape, 0) & (nq - 1))
    causal = col <= qpos
    s = jnp.where(causal, s, NEG)
    s = jnp.where(col < past + t_new, s, -jnp.inf)
    _selection_bias(s, causal, n_sel, LANES, bias_ref)
    mask_ref[...] = jnp.where(bias_ref[...] > -1.0, 1.0, 0.0)


def _sel_sample(page_table, layer, qh, ql, w, knew, cache_ki, t_new):
    b, n_pages = page_table.shape
    past = n_pages * PAGE
    n_sel = min(TOPK, (past + t_new) // 4)
    rows = qh.shape[1]
    nq = rows // IDX_HEADS
    assert nq & (nq - 1) == 0
    width = past + PAGE
    steps = n_pages // SEL_PAGES
    cache_ki = jnp.swapaxes(cache_ki, 2, 3)
    knew = jnp.swapaxes(knew, 1, 2)
    page_spec = lambda u: pl.BlockSpec((1, 1, IDX_DIM, PAGE),
                                       lambda bi, s, pt: (layer, pt[bi, s * SEL_PAGES + u], 0, 0))
    per_b = lambda shp: pl.BlockSpec((1,) + shp, lambda bi, s, pt: (bi, 0, 0))
    scores = pl.pallas_call(
        functools.partial(_score_sample_kernel, past=past),
        out_shape=jax.ShapeDtypeStruct((b, 8, width), F32),
        grid_spec=pltpu.PrefetchScalarGridSpec(
            num_scalar_prefetch=1,
            grid=(b, steps),
            in_specs=[per_b((rows, IDX_DIM)), per_b((rows, IDX_DIM)), per_b((rows, 1)), per_b((IDX_DIM, PAGE))]
                     + [page_spec(u) for u in range(SEL_PAGES)],
            out_specs=per_b((8, width)),
        ),
        compiler_params=_cparams(("parallel", "arbitrary")),
        name="score_sample",
    )(page_table, qh, ql, w, knew, *([cache_ki] * SEL_PAGES))
    flat = scores[:, :nq].reshape(b * nq, width)
    mask = pl.pallas_call(
        functools.partial(_select_sample_kernel, n_sel=n_sel, past=past, t_new=t_new, nq=nq),
        out_shape=jax.ShapeDtypeStruct((b * nq, width), F32),
        scratch_shapes=[pltpu.VMEM((b * nq, width), F32)],
        compiler_params=pltpu.CompilerParams(vmem_limit_bytes=VMEM_LIMIT),
        name="select_sample",
    )(flat)
    return jnp.pad(mask.reshape(b, nq, width), ((0, 0), (0, MASK_ROWS - nq), (0, 0)))


def _attn_sample_kernel(pt_ref, q_ref, knew_ref, vnew_ref, mnew_ref, mask_ref, *rest, nq):
    kpages = rest[:ATT_PAGES]
    vpages = rest[ATT_PAGES:2 * ATT_PAGES]
    o_ref = rest[2 * ATT_PAGES]
    m_ref, l_ref, acc_ref = rest[2 * ATT_PAGES + 1:]
    step = pl.program_id(1)
    q = q_ref[0]
    rows = q.shape[0]
    span = PAGE * N_HEADS_A
    tok = lax.broadcasted_iota(I32, (PAGE, span), 0)
    ln = lax.broadcasted_iota(I32, (PAGE, span), 1)
    expand = jnp.where(lax.shift_right_logical(ln, 3) == tok, 1.0, 0.0).astype(BF16)

    def valid_of(mk, width):
        e = _dot(mk.astype(BF16), expand[:, :width])
        sel = jnp.broadcast_to(e[:nq][:, None, :], (nq, N_HEADS_A, width)).reshape(rows, width) > 0.5
        lane_h = lax.broadcasted_iota(I32, (rows, width), 1) & (N_HEADS_A - 1)
        row_h = lax.broadcasted_iota(I32, (rows, width), 0) & (N_HEADS_A - 1)
        return jnp.logical_and(sel, lane_h == row_h)

    def update(kb, vb, valid, first):
        s = jnp.where(valid, _dot_nt(q, kb), NEG)
        m_blk = jnp.max(s, axis=1, keepdims=True)
        if first:
            m_new = m_blk
            p = jnp.where(valid, jnp.exp(s - m_new), 0.0)
            l_ref[...] = jnp.sum(p, axis=1, keepdims=True)
            acc_ref[...] = _dot(p.astype(BF16), vb)
        else:
            m_old = m_ref[...]
            m_new = jnp.maximum(m_old, m_blk)
            alpha = jnp.exp(m_old - m_new)
            p = jnp.where(valid, jnp.exp(s - m_new), 0.0)
            l_ref[...] = alpha * l_ref[...] + jnp.sum(p, axis=1, keepdims=True)
            acc_ref[...] = alpha * acc_ref[...] + _dot(p.astype(BF16), vb)
        m_ref[...] = m_new

    @pl.when(step == 0)
    def _():
        update(knew_ref[0], vnew_ref[0], valid_of(mnew_ref[0], PAGE), True)

    @pl.when(step > 0)
    def _():
        kb = jnp.concatenate([pg[0, 0].reshape(span, HEAD_A).astype(BF16) for pg in kpages], axis=0)
        vb = jnp.concatenate([pg[0, 0].reshape(span, HEAD_A).astype(BF16) for pg in vpages], axis=0)
        valid = jnp.concatenate([valid_of(mask_ref[0, :, u * PAGE:(u + 1) * PAGE], span)
                                 for u in range(ATT_PAGES)], axis=1)
        update(kb, vb, valid, False)

    @pl.when(step == pl.num_programs(1) - 1)
    def _():
        o_ref[0] = acc_ref[...] / l_ref[...]


def _attn_sample(page_table, layer, q, knew, vnew, mask, cache_k, cache_v):
    b, n_pages = page_table.shape
    rows = q.shape[1]
    nq = rows // N_HEADS_A
    past = n_pages * PAGE
    steps = n_pages // ATT_PAGES + 1
    pg = lambda s, u: jnp.maximum(s - 1, 0) * ATT_PAGES + u
    page_spec = lambda u: pl.BlockSpec((1, 1, PAGE, N_HEADS_A, HEAD_A),
                                       lambda bi, s, pt: (layer, pt[bi, pg(s, u)], 0, 0, 0))
    per_b = lambda shp: pl.BlockSpec((1,) + shp, lambda bi, s, pt: (bi, 0, 0))
    return pl.pallas_call(
        functools.partial(_attn_sample_kernel, nq=nq),
        out_shape=jax.ShapeDtypeStruct((b, rows, HEAD_A), F32),
        grid_spec=pltpu.PrefetchScalarGridSpec(
            num_scalar_prefetch=1,
            grid=(b, steps),
            in_specs=[per_b((rows, HEAD_A)), per_b((PAGE, HEAD_A)), per_b((PAGE, HEAD_A)),
                      pl.BlockSpec((1, MASK_ROWS, PAGE), lambda bi, s, pt: (bi, 0, past // PAGE)),
                      pl.BlockSpec((1, MASK_ROWS, ATT_PAGES * PAGE),
                                   lambda bi, s, pt: (bi, 0, jnp.maximum(s - 1, 0)))]
                     + [page_spec(u) for u in range(ATT_PAGES)] * 2,
            out_specs=per_b((rows, HEAD_A)),
            scratch_shapes=[pltpu.VMEM((rows, 1), F32), pltpu.VMEM((rows, 1), F32),
                            pltpu.VMEM((rows, HEAD_A), F32)],
        ),
        compiler_params=_cparams(("parallel", "arbitrary")),
        name="attn_sample",
    )(page_table, q, knew, vnew, mask, mask, *([cache_k] * ATT_PAGES), *([cache_v] * ATT_PAGES))


def _seg_sum(x, jmat):
    r = x.shape[0]
    xs = jnp.concatenate([x[:, p * LANES:(p + 1) * LANES] for p in range(N_PAIRS)], axis=0)
    hi, lo = _split_bf16(xs)
    ys = _dot(hi, jmat) + _dot(lo, jmat)
    return jnp.concatenate([ys[p * r:(p + 1) * r] for p in range(N_PAIRS)], axis=1)


def _rwkv_kernel(pb_ref, first_ref, s0_ref, mu_ref, w0_ref, a0_ref, kk_ref, ka_ref, rk_ref, lw_ref, lb_ref,
                 w2h, w2l, a2h, a2l, g2h, g2l, ob_ref, st_ref, carry_ref, state_ref, *, t_valid):
    nb, c = pb_ref.shape[0], pb_ref.shape[1]
    rows = nb * c
    n = pl.program_id(1)
    half = HEAD_B

    @pl.when(n == 0)
    def _():
        carry_ref[...] = first_ref[:, 0, :]
        z = jnp.zeros((half, half), F32)
        for b in range(nb):
            for p in range(N_PAIRS):
                top = jnp.concatenate([s0_ref[b, 2 * p], z], axis=1)
                bot = jnp.concatenate([z, s0_ref[b, 2 * p + 1]], axis=1)
                state_ref[b * N_PAIRS + p] = jnp.concatenate([top, bot], axis=0)

    pb = pb_ref[...].reshape(rows, PB_W)
    rowi = lax.broadcasted_iota(I32, (rows, 1), 0)
    step_i = rowi & (c - 1)
    carry = jnp.broadcast_to(carry_ref[...][:, None, :], (nb, c, PB_W)).reshape(rows, PB_W)
    prev = jnp.where(step_i == 0, carry, pltpu.roll(pb, 1, axis=0))
    carry_ref[...] = pb_ref[:, c - 1, :]
    xm = pb + (prev - pb) * mu_ref[...]

    r = xm[:, 0:C_B]
    k = xm[:, C_B:2 * C_B]
    v = xm[:, 2 * C_B:3 * C_B]
    xw = xm[:, 3 * C_B:3 * C_B + 128]
    xa = xm[:, 3 * C_B + 128:3 * C_B + 256]
    xg = xm[:, 3 * C_B + 256:3 * C_B + 512]
    w = -jax.nn.softplus(-(w0_ref[...] + _dot3w(jnp.tanh(xw), w2h, w2l))) - 0.5
    ld = -jnp.exp(w)
    a = jax.nn.sigmoid(a0_ref[...] + _dot3w(xa, a2h, a2l))
    g = _dot3w(jax.nn.sigmoid(xg), g2h, g2l)

    li = lax.broadcasted_iota(I32, (LANES, LANES), 0)
    lj = lax.broadcasted_iota(I32, (LANES, LANES), 1)
    same_head = (li // half) == (lj // half)
    jmat = jnp.where(same_head, 1.0, 0.0).astype(BF16)

    kkr = k * kk_ref[...]
    kk = kkr * lax.rsqrt(jnp.maximum(_seg_sum(kkr * kkr, jmat), 1e-24))
    k2 = k * (1.0 + (a - 1.0) * ka_ref[...])
    bonus = _seg_sum(r * k2 * rk_ref[...], jmat)
    avec = -kk
    bvec = kk * a
    if t_valid < c:
        live = step_i < t_valid
        ld = jnp.where(live, ld, 0.0)
        avec = jnp.where(live, avec, 0.0)
        bvec = jnp.where(live, bvec, 0.0)
        k2s = jnp.where(live, k2, 0.0)
        vs = jnp.where(live, v, 0.0)
    else:
        k2s, vs = k2, v

    ti = lax.broadcasted_iota(I32, (rows, rows), 0)
    tj = lax.broadcasted_iota(I32, (rows, rows), 1)
    tri = jnp.where(jnp.logical_and(tj <= ti, (ti // c) == (tj // c)), 1.0, 0.0).astype(BF16)
    l1 = ld.astype(BF16)
    rem = ld - l1.astype(F32)
    l2 = rem.astype(BF16)
    l3 = (rem - l2.astype(F32)).astype(BF16)
    logc = _dot(tri, l1) + _dot(tri, l2) + _dot(tri, l3)
    logc_end = jnp.broadcast_to(logc.reshape(nb, c, C_B)[:, c - 1:c, :], (nb, c, C_B)).reshape(rows, C_B)
    e_pos = jnp.exp(logc)
    e_neg = jnp.exp(-logc)
    e_end = jnp.exp(logc_end - logc)

    def grp(x):
        return jnp.stack([x[b * c:(b + 1) * c, p * LANES:(p + 1) * LANES]
                          for b in range(nb) for p in range(N_PAIRS)], axis=0)

    rt = grp(r * e_pos)
    at = grp(avec * jnp.exp(logc - ld))
    kt = grp(k2s * e_neg)
    bt = grp(bvec * e_neg)
    kc = grp(k2s * e_end)
    bc = grp(bvec * e_end)
    vg = grp(vs)
    dec_end = grp(jnp.exp(logc_end))[:, 0:1, :]

    lane = lax.broadcasted_iota(I32, (1, 1, LANES), 2)
    m0 = lane < half
    col4 = lax.broadcasted_iota(I32, (1, c, 4 * c), 2) & (c - 1)
    row4 = lax.broadcasted_iota(I32, (1, c, 4 * c), 1)
    eye2 = jnp.where(lax.broadcasted_iota(I32, (1, 2 * c, 2 * c), 1)
                     == lax.broadcasted_iota(I32, (1, 2 * c, 2 * c), 2), 1.0, 0.0)
    lane2c = lax.broadcasted_iota(I32, (1, 1, 2 * c), 2) < c

    def two(x):
        return jnp.concatenate([jnp.where(m0, x, 0.0), jnp.where(m0, 0.0, x)], axis=1)

    s_old = state_ref[...]
    lhs1 = jnp.concatenate([rt, at], axis=1).astype(BF16)
    rhs1 = jnp.concatenate([two(kt), two(bt)], axis=1).astype(BF16)
    amat = _bmm_nt(lhs1, rhs1)
    a_r = jnp.where(col4 <= row4, amat[:, :c], 0.0)
    a_a = jnp.where(col4 < row4, amat[:, c:], 0.0)
    x0 = _bmm_nt(lhs1, s_old.astype(BF16))
    vm = two(vg).astype(BF16)
    x = x0[:, c:] + _bmm(a_a[:, :, :2 * c].astype(BF16), vm)
    lab = a_a[:, :, 2 * c:]
    lbd = jnp.concatenate([jnp.where(lane2c, lab, 0.0), jnp.where(lane2c, 0.0, lab)], axis=1)
    pm = eye2 + lbd
    lk = lbd
    for _ in range(int(np.log2(c)) - 1):
        lkb = lk.astype(BF16)
        lk = _bmm(lkb, lkb)
        pm = pm + _bmm(pm.astype(BF16), lk.astype(BF16))
    us = _bmm(pm.astype(BF16), two(x).astype(BF16))
    u = us[:, :c] + us[:, c:]
    y = x0[:, :c] + _bmm(a_r.astype(BF16), jnp.concatenate([vm, two(u).astype(BF16)], axis=1))
    vals = jnp.concatenate([vg, u], axis=1)
    keys = jnp.concatenate([kc, bc], axis=1).astype(BF16)
    add = _bmm(jnp.swapaxes(vals, 1, 2).astype(BF16), keys)
    state_ref[...] = s_old * dec_end + jnp.where(same_head[None], add, 0.0)

    y = jnp.concatenate([jnp.concatenate([y[b * N_PAIRS + p] for p in range(N_PAIRS)], axis=1)
                         for b in range(nb)], axis=0)
    mean = _seg_sum(y, jmat) * (1.0 / half)
    d = y - mean
    var = _seg_sum(d * d, jmat) * (1.0 / half)
    yn = d * lax.rsqrt(var + LNX_EPS) * lw_ref[...] + lb_ref[...]
    ob_ref[...] = ((yn + bonus * v) * g).astype(ob_ref.dtype).reshape(nb, c, C_B)

    @pl.when(n == pl.num_programs(1) - 1)
    def _():
        for b in range(nb):
            for p in range(N_PAIRS):
                s_new = state_ref[b * N_PAIRS + p]
                st_ref[b, 2 * p] = s_new[:half, :half]
                st_ref[b, 2 * p + 1] = s_new[half:, half:]


def _rwkv(pb, first, s0, vecs, loras, t_valid):
    b, t, _ = pb.shape
    nb = int(np.gcd(b, RWKV_NB))
    vec = lambda a: pl.BlockSpec(a.shape, lambda bi, n: (0, 0))
    st_spec = pl.BlockSpec((nb,) + s0.shape[1:], lambda bi, n: (bi, 0, 0, 0))
    return pl.pallas_call(
        functools.partial(_rwkv_kernel, t_valid=t_valid),
        out_shape=(jax.ShapeDtypeStruct((b, t, C_B), BF16), jax.ShapeDtypeStruct(s0.shape, F32)),
        grid=(b // nb, t // CHUNK),
        in_specs=[pl.BlockSpec((nb, CHUNK, PB_W), lambda bi, n: (bi, n, 0)),
                  pl.BlockSpec((nb, 1, PB_W), lambda bi, n: (bi, 0, 0)), st_spec]
                 + [vec(a) for a in vecs] + [vec(a) for a in loras],
        out_specs=(pl.BlockSpec((nb, CHUNK, C_B), lambda bi, n: (bi, n, 0)), st_spec),
        scratch_shapes=[pltpu.VMEM((nb, PB_W), F32), pltpu.VMEM((nb * N_PAIRS, LANES, LANES), F32)],
        compiler_params=_cparams(("parallel", "arbitrary")),
        name="rwkv7",
    )(pb, first, s0, *vecs, *loras)


def _merge_kernel(oa_ref, ob_ref, wpa_ref, wpb_ref, ga_ref, gb_ref, o_ref):
    za = _dot(oa_ref[...], wpa_ref[...])
    zb = _dot(ob_ref[...], wpb_ref[...])
    o_ref[...] = (ga_ref[...].astype(F32) * za + gb_ref[...].astype(F32) * zb).astype(o_ref.dtype)


def _merge(oa, ob, wpa, wpb, gates, tn=1024):
    m = oa.shape[0]
    tm = _row_tile(m, 1024)
    nb = D_MODEL // tn
    return pl.pallas_call(
        _merge_kernel,
        out_shape=jax.ShapeDtypeStruct((m, D_MODEL), BF16),
        grid=(m // tm, nb),
        in_specs=[pl.BlockSpec((tm, C_A), lambda i, j: (i, 0)), pl.BlockSpec((tm, C_B), lambda i, j: (i, 0)),
                  pl.BlockSpec((C_A, tn), lambda i, j: (0, j)), pl.BlockSpec((C_B, tn), lambda i, j: (0, j)),
                  pl.BlockSpec((tm, tn), lambda i, j: (i, j)), pl.BlockSpec((tm, tn), lambda i, j: (i, j + nb))],
        out_specs=pl.BlockSpec((tm, tn), lambda i, j: (i, j)),
        compiler_params=_cparams(("parallel", "parallel")),
        name="merge",
    )(oa, ob, wpa, wpb, gates, gates)


def _oproj_kernel(x_ref, mix_ref, wo_ref, g_ref, x1_ref, h2_ref):
    x1 = x_ref[...] + _dot(mix_ref[...], wo_ref[...])
    x1_ref[...] = x1
    h2 = x1 * lax.rsqrt(jnp.mean(x1 * x1, axis=-1, keepdims=True) + NORM_EPS) * g_ref[...]
    h2_ref[...] = h2.astype(BF16)


def _oproj(x, mixed, wo, g):
    m, d = x.shape
    tm = _row_tile(m)
    row = pl.BlockSpec((tm, d), lambda i: (i, 0))
    return pl.pallas_call(
        _oproj_kernel,
        out_shape=(jax.ShapeDtypeStruct((m, d), F32), jax.ShapeDtypeStruct((m, d), BF16)),
        grid=(m // tm,),
        in_specs=[row, row, pl.BlockSpec((d, d), lambda i: (0, 0)), pl.BlockSpec((1, d), lambda i: (0, 0))],
        out_specs=(row, row),
        compiler_params=_cparams(("parallel",)),
        name="oproj_norm",
    )(x, mixed, wo, g.reshape(1, d))


def _ffn_up_kernel(h_ref, wg_ref, wu_ref, o_ref, wg_bf, wu_bf):
    @pl.when(pl.program_id(1) == 0)
    def _():
        wg_bf[...] = wg_ref[...].astype(BF16)
        wu_bf[...] = wu_ref[...].astype(BF16)

    h = h_ref[...]
    gate = _dot(h, wg_bf[...])
    up = _dot(h, wu_bf[...])
    o_ref[...] = (gate * jax.nn.sigmoid(gate) * up).astype(o_ref.dtype)


def _ffn_up(h2, wg, wu, tn=512):
    m, d = h2.shape
    n = wg.shape[1]
    tm = _row_tile(m, 1024)
    return pl.pallas_call(
        _ffn_up_kernel,
        out_shape=jax.ShapeDtypeStruct((m, n), BF16),
        grid=(n // tn, m // tm),
        in_specs=[pl.BlockSpec((tm, d), lambda j, i: (i, 0)), pl.BlockSpec((d, tn), lambda j, i: (0, j)),
                  pl.BlockSpec((d, tn), lambda j, i: (0, j))],
        out_specs=pl.BlockSpec((tm, tn), lambda j, i: (i, j)),
        scratch_shapes=[pltpu.VMEM((d, tn), BF16), pltpu.VMEM((d, tn), BF16)],
        compiler_params=_cparams(("arbitrary", "arbitrary")),
        name="ffn_up",
    )(h2, wg, wu)


def _ffn_down_kernel(x_ref, a_ref, wd_ref, o_ref, wd_bf):
    @pl.when(pl.program_id(1) == 0)
    def _():
        wd_bf[...] = wd_ref[...].astype(BF16)

    o_ref[...] = x_ref[...] + _dot(a_ref[...], wd_bf[...])


def _ffn_down(x1, act, wd, tn=512):
    m, d = x1.shape
    kdim = act.shape[1]
    tm = _row_tile(m)
    return pl.pallas_call(
        _ffn_down_kernel,
        out_shape=jax.ShapeDtypeStruct((m, d), F32),
        grid=(d // tn, m // tm),
        in_specs=[pl.BlockSpec((tm, tn), lambda j, i: (i, j)), pl.BlockSpec((tm, kdim), lambda j, i: (i, 0)),
                  pl.BlockSpec((kdim, tn), lambda j, i: (0, j))],
        out_specs=pl.BlockSpec((tm, tn), lambda j, i: (i, j)),
        scratch_shapes=[pltpu.VMEM((kdim, tn), BF16)],
        compiler_params=_cparams(("arbitrary", "arbitrary")),
        name="ffn_down",
    )(x1, act, wd)


def _pad_cols(w, width):
    return jnp.pad(w, ((0, 0), (0, width - w.shape[1])))


def _pad_rows(w, height):
    return jnp.pad(w, ((0, height - w.shape[0]), (0, 0)))


def _prep_weights(w_in, mu_b, w2, a2, g2):
    o = 0
    cols = {}
    for name, wdt in (("q", C_A), ("k", C_A), ("v", C_A), ("qi", IDX_HEADS * IDX_DIM), ("ki", IDX_DIM),
                      ("wi", IDX_HEADS), ("g", 2 * D_MODEL), ("rkv", 3 * C_B), ("xw", D_DECAY), ("xa", D_AAA),
                      ("xg", D_GATE)):
        cols[name] = (o, o + wdt)
        o += wdt
    cut = lambda name: w_in[:, cols[name][0]:cols[name][1]]
    w_idx = _pad_cols(jnp.concatenate([cut("qi"), cut("ki"), cut("wi")], axis=1), IDX_W)
    idx_hi, idx_lo = _split_bf16(w_idx)
    w_b = jnp.concatenate([cut("rkv"), _pad_cols(cut("xw"), 128), _pad_cols(cut("xa"), 128),
                           _pad_cols(cut("xg"), 256)], axis=1)
    nb0 = cols["rkv"][0]
    mu = lambda name: mu_b[cols[name][0] - nb0:cols[name][1] - nb0][None, :]
    mu_p = jnp.concatenate([mu("rkv"), _pad_cols(mu("xw"), 128), _pad_cols(mu("xa"), 128),
                            _pad_cols(mu("xg"), 256)], axis=1)
    loras = []
    for wl, hgt in ((w2, 128), (a2, 128), (g2, 256)):
        loras += list(_split_bf16(_pad_rows(wl, hgt)))
    return dict(q=cut("q").astype(BF16), k=cut("k").astype(BF16), v=cut("v").astype(BF16),
                idx_hi=idx_hi, idx_lo=idx_lo, g=cut("g").astype(BF16), b=w_b.astype(BF16), mu=mu_p,
                loras=loras)


def _layer(x, pos, wts, p, *, seq, attn_fn, first_pb, s0, t_valid):
    m = x.shape[0]
    nb = m // seq
    hh, hl = _norm_split(x, p["g_mix"])
    tabs_a = _rope_tables(pos, HEAD_A)
    tabs_i = _rope_tables(pos, IDX_DIM)
    (q_b,) = _proj(hh, wts["q"], tabs_a, rope=True, want_f32=False, want_bf16=True, scale=HEAD_A ** -0.5)
    k_f, k_b = _proj(hh, wts["k"], tabs_a, rope=True, want_f32=True, want_bf16=True)
    v_f, v_b = _proj(hh, wts["v"], tabs_a, rope=False, want_f32=True, want_bf16=True)
    qcat, kcat, ki_f, wi = _idx_proj(hh, hl, wts["idx_hi"], wts["idx_lo"], tabs_i)
    gates = _mm(hh, wts["g"], tn=1024, out_dtype=BF16, act="sigmoid")
    pb = _mm(hh, wts["b"], tn=PB_W // 4, out_dtype=F32)

    o_a = attn_fn(q_b, k_b, v_b, qcat, kcat, ki_f, wi)

    t_pad = -(-seq // CHUNK) * CHUNK
    pb3 = pb.reshape(nb, seq, PB_W)
    if t_pad != seq:
        pb3 = jnp.pad(pb3, ((0, 0), (0, t_pad - seq), (0, 0)))
    vecs = [wts["mu"]] + [p[n].reshape(1, C_B) for n in ("w0", "a0", "k_k", "k_a", "r_k", "lnx_w", "lnx_b")]
    o_b, wkv = _rwkv(pb3, first_pb, s0, vecs, wts["loras"], t_valid)
    o_b = o_b[:, :seq].reshape(m, C_B)

    mixed = _merge(o_a, o_b, p["w_pa"], p["w_pb"], gates)
    x1, h2 = _oproj(x, mixed, p["w_o"], p["g_ffn"])
    act = _ffn_up(h2, p["w_gate"], p["w_up"])
    x2 = _ffn_down(x1, act, p["w_down"])
    shift = _norm_f32(x.reshape(nb, seq, D_MODEL)[:, -1], p["g_mix"])
    return x2, k_f, v_f, ki_f, wkv, shift


def kernel(x_prompt, x_sample, cache_k, cache_v, cache_kidx, state_wkv, state_shift, page_table, g_mix, w_in, mu_b, w0, w2, a0, a2, g2, k_k, k_a, r_k, lnx_w, lnx_b, w_pa, w_pb, w_o, g_ffn, w_gate, w_up, w_down, g_final):
    bp, sp, d = x_prompt.shape
    bs, ts, _ = x_sample.shape
    depth = w_in.shape[0]
    n_pages = page_table.shape[1]
    past = n_pages * PAGE
    xp = x_prompt.reshape(bp * sp, d)
    xs = x_sample.reshape(bs * ts, d)
    outs_p, outs_s = [], []
    for l in range(depth):
        wts = _prep_weights(w_in[l], mu_b[l], w2[l], a2[l], g2[l])
        p = dict(g_mix=g_mix[l], w0=w0[l], a0=a0[l], k_k=k_k[l], k_a=k_a[l], r_k=r_k[l], lnx_w=lnx_w[l],
                 lnx_b=lnx_b[l], w_pa=w_pa[l].astype(BF16), w_pb=w_pb[l].astype(BF16), w_o=w_o[l].astype(BF16),
                 g_ffn=g_ffn[l], w_gate=w_gate[l], w_up=w_up[l], w_down=w_down[l])

        def attn_p(q_b, k_b, v_b, qcat, kcat, ki_f, wi):
            r3 = lambda a: a.reshape(bp, sp, a.shape[-1])
            return _attn_prompt(r3(q_b), r3(k_b), r3(v_b), r3(qcat), r3(kcat), r3(wi)).reshape(bp * sp, C_A)

        def attn_s(q_b, k_b, v_b, qcat, kcat, ki_f, wi):
            qc = qcat.reshape(bs, ts * IDX_HEADS, 256)
            qh, ql = qc[..., :IDX_DIM], qc[..., IDX_DIM:2 * IDX_DIM]
            wrow = wi.reshape(bs, ts * IDX_HEADS, 1)
            ki_new = jnp.pad(ki_f.reshape(bs, ts, IDX_DIM), ((0, 0), (0, PAGE - ts), (0, 0)))
            mask = _sel_sample(page_table, l, qh, ql, wrow, ki_new, cache_kidx, ts)
            as_rows = lambda a: a.reshape(bs, ts * N_HEADS_A, HEAD_A)
            pad_rows = lambda a: jnp.pad(as_rows(a), ((0, 0), (0, PAGE - ts * N_HEADS_A), (0, 0)))
            o = _attn_sample(page_table, l, as_rows(q_b), pad_rows(k_b), pad_rows(v_b), mask, cache_k, cache_v)
            return o.reshape(bs * ts, C_A).astype(BF16)

        zero_first = jnp.zeros((bp, 1, PB_W), F32)
        zero_state = jnp.zeros((bp, N_HEADS_B, HEAD_B, HEAD_B), F32)
        xp, kp, vp, kip, wkvp, shp = _layer(xp, jnp.arange(sp), wts, p, seq=sp, attn_fn=attn_p,
                                            first_pb=zero_first, s0=zero_state, t_valid=CHUNK)
        first_s = _mm(state_shift[l].astype(BF16), wts["b"], tn=PB_W // 4, out_dtype=F32).reshape(bs, 1, PB_W)
        pos_s = jnp.tile(past + jnp.arange(ts), bs)
        xs, ksm, vsm, kis, wkvs, shs = _layer(xs, pos_s, wts, p, seq=ts, attn_fn=attn_s,
                                              first_pb=first_s, s0=state_wkv[l], t_valid=ts)
        outs_p.append((kp.reshape(bp, sp, N_HEADS_A, HEAD_A), vp.reshape(bp, sp, N_HEADS_A, HEAD_A),
                       kip.reshape(bp, sp, IDX_DIM), wkvp, shp))
        outs_s.append((ksm.reshape(bs, ts, N_HEADS_A, HEAD_A), vsm.reshape(bs, ts, N_HEADS_A, HEAD_A),
                       kis.reshape(bs, ts, IDX_DIM), wkvs, shs))
    y_prompt = _norm_f32(xp, g_final).reshape(bp, sp, d)
    y_sample = _norm_f32(xs, g_final).reshape(bs, ts, d)
    stack = lambda outs, i: jnp.stack([o[i] for o in outs])
    return ((y_prompt, y_sample) + tuple(stack(outs_p, i) for i in range(5))
            + tuple(stack(outs_s, i) for i in range(5)))
```

```python
import functools

import numpy as np
import jax
import jax.numpy as jnp
from jax import lax
from jax.experimental import pallas as pl
from jax.experimental.pallas import tpu as pltpu

F32 = jnp.float32
BF16 = jnp.bfloat16
I32 = jnp.int32
I16 = jnp.int16

D_MODEL = 2048
HEAD_A = 128
N_HEADS_A = 8
C_A = N_HEADS_A * HEAD_A
IDX_HEADS = 8
IDX_DIM = 64
TOPK = 256
HEAD_B = 64
N_HEADS_B = 16
C_B = N_HEADS_B * HEAD_B
D_DECAY = 64
D_AAA = 64
D_GATE = 160
LNX_EPS = 64e-5
ROPE_THETA = 500000.0
NORM_EPS = 1e-6
PAGE = 128
NEG = -1e30
INT_MIN = -(2 ** 31)

LANES = 128
VMEM_LIMIT = 56 * 1024 * 1024
CHUNK = 64
RWKV_NB = 4
N_PAIRS = N_HEADS_B // 2
TQ = 256
RANK_BLK = 256
MASK_ROWS = 16
PB_W = 3 * C_B + 128 + 128 + 256
IDX_W = 640


def _cparams(sem):
    return pltpu.CompilerParams(dimension_semantics=sem, vmem_limit_bytes=VMEM_LIMIT)


def _split_bf16(x):
    hi = x.astype(BF16)
    lo = (x - hi.astype(F32)).astype(BF16)
    return hi, lo


def _dot(a, b):
    return jnp.dot(a, b, preferred_element_type=F32)


def _dot_nt(a, b):
    return lax.dot_general(a, b, (((1,), (1,)), ((), ())), preferred_element_type=F32)


def _bmm(a, b):
    return lax.dot_general(a, b, (((2,), (1,)), ((0,), (0,))), preferred_element_type=F32)


def _bmm_nt(a, b):
    return lax.dot_general(a, b, (((2,), (2,)), ((0,), (0,))), preferred_element_type=F32)


def _dot3w(x, wh_ref, wl_ref):
    xh, xl = _split_bf16(x)
    wh = wh_ref[...]
    return _dot(xh, wh) + _dot(xl, wh) + _dot(xh, wl_ref[...])


def _norm_split_kernel(x_ref, g_ref, hi_ref, lo_ref):
    x = x_ref[...]
    y = x * lax.rsqrt(jnp.mean(x * x, axis=-1, keepdims=True) + NORM_EPS) * g_ref[...]
    hi = y.astype(BF16)
    hi_ref[...] = hi
    lo_ref[...] = (y - hi.astype(F32)).astype(BF16)


def _norm_f32_kernel(x_ref, g_ref, o_ref):
    x = x_ref[...]
    o_ref[...] = x * lax.rsqrt(jnp.mean(x * x, axis=-1, keepdims=True) + NORM_EPS) * g_ref[...]


def _row_tile(m, pref=512):
    for tm in (pref, 512):
        if m % tm == 0:
            return tm
    return m


def _norm_split(x, g):
    m, d = x.shape
    tm = _row_tile(m)
    return pl.pallas_call(
        _norm_split_kernel,
        out_shape=(jax.ShapeDtypeStruct((m, d), BF16), jax.ShapeDtypeStruct((m, d), BF16)),
        grid=(m // tm,),
        in_specs=[pl.BlockSpec((tm, d), lambda i: (i, 0)), pl.BlockSpec((1, d), lambda i: (0, 0))],
        out_specs=(pl.BlockSpec((tm, d), lambda i: (i, 0)), pl.BlockSpec((tm, d), lambda i: (i, 0))),
        compiler_params=_cparams(("parallel",)),
        name="norm_split",
    )(x, g.reshape(1, d))


def _norm_f32(x, g):
    m, d = x.shape
    tm = _row_tile(m)
    return pl.pallas_call(
        _norm_f32_kernel,
        out_shape=jax.ShapeDtypeStruct((m, d), F32),
        grid=(m // tm,),
        in_specs=[pl.BlockSpec((tm, d), lambda i: (i, 0)), pl.BlockSpec((1, d), lambda i: (0, 0))],
        out_specs=pl.BlockSpec((tm, d), lambda i: (i, 0)),
        compiler_params=_cparams(("parallel",)),
        name="norm_f32",
    )(x, g.reshape(1, d))


def _rope_tables(pos, head_dim):
    rot = head_dim // 4
    half = rot // 2
    inv = ROPE_THETA ** (-jnp.arange(half, dtype=F32) * 2.0 / rot)
    ang = pos.astype(F32)[:, None] * inv[None, :]
    cos, sin = jnp.cos(ang), jnp.sin(ang)
    t = pos.shape[0]
    c = jnp.ones((t, head_dim), F32).at[:, :half].set(cos).at[:, half:rot].set(cos)
    s1 = jnp.zeros((t, head_dim), F32).at[:, half:rot].set(sin)
    s2 = jnp.zeros((t, head_dim), F32).at[:, :half].set(-sin)
    rep = LANES // head_dim
    return tuple(jnp.tile(a, (1, rep)) for a in (c, s1, s2))


def _rope_lanes(x, c, s1, s2, half):
    return x * c + pltpu.roll(x, half, axis=1) * s1 + pltpu.roll(x, LANES - half, axis=1) * s2


def _proj_kernel(a_ref, w_ref, c_ref, s1_ref, s2_ref, *out_refs, rope, want_f32, want_bf16, scale):
    acc = _dot(a_ref[...], w_ref[...])
    if rope:
        c, s1, s2 = c_ref[...], s1_ref[...], s2_ref[...]
        parts = [_rope_lanes(acc[:, h * LANES:(h + 1) * LANES], c, s1, s2, HEAD_A // 8)
                 for h in range(acc.shape[1] // LANES)]
        acc = jnp.concatenate(parts, axis=1)
    if scale is not None:
        acc = acc * scale
    k = 0
    if want_f32:
        out_refs[k][...] = acc
        k += 1
    if want_bf16:
        out_refs[k][...] = acc.astype(BF16)


def _proj(a, w, tabs, *, rope, want_f32, want_bf16, scale=None):
    m, kdim = a.shape
    n = w.shape[1]
    tm = _row_tile(m, 1024)
    ntab = tabs[0].shape[0] // tm
    outs, ospecs = [], []
    for want, dt in ((want_f32, F32), (want_bf16, BF16)):
        if want:
            outs.append(jax.ShapeDtypeStruct((m, n), dt))
            ospecs.append(pl.BlockSpec((tm, n), lambda i: (i, 0)))
    tab_spec = pl.BlockSpec((tm, LANES), lambda i: (i % ntab, 0))
    res = pl.pallas_call(
        functools.partial(_proj_kernel, rope=rope, want_f32=want_f32, want_bf16=want_bf16, scale=scale),
        out_shape=tuple(outs),
        grid=(m // tm,),
        in_specs=[pl.BlockSpec((tm, kdim), lambda i: (i, 0)), pl.BlockSpec((kdim, n), lambda i: (0, 0)),
                  tab_spec, tab_spec, tab_spec],
        out_specs=tuple(ospecs),
        compiler_params=_cparams(("parallel",)),
        name="proj_rope" if rope else "proj",
    )(a, w, *tabs)
    return res


def _mm_kernel(a_ref, w_ref, o_ref, *, act):
    acc = _dot(a_ref[...], w_ref[...])
    if act == "sigmoid":
        acc = jax.nn.sigmoid(acc)
    o_ref[...] = acc.astype(o_ref.dtype)


def _mm(a, w, *, tn, out_dtype, act=None):
    m, kdim = a.shape
    n = w.shape[1]
    tm = _row_tile(m, 1024)
    return pl.pallas_call(
        functools.partial(_mm_kernel, act=act),
        out_shape=jax.ShapeDtypeStruct((m, n), out_dtype),
        grid=(m // tm, n // tn),
        in_specs=[pl.BlockSpec((tm, kdim), lambda i, j: (i, 0)), pl.BlockSpec((kdim, tn), lambda i, j: (0, j))],
        out_specs=pl.BlockSpec((tm, tn), lambda i, j: (i, j)),
        compiler_params=_cparams(("parallel", "parallel")),
        name="mm_" + (act or "plain"),
    )(a, w)


def _idx_proj_kernel(hh_ref, hl_ref, wh_ref, wl_ref, c_ref, s1_ref, s2_ref,
                     qcat_ref, kcat_ref, kf_ref, wi_ref):
    hh = hh_ref[...]
    wh = wh_ref[...]
    acc = _dot(hh, wh) + _dot(hl_ref[...], wh) + _dot(hh, wl_ref[...])
    c, s1, s2 = c_ref[...], s1_ref[...], s2_ref[...]
    half = IDX_DIM // 8
    zeros = jnp.zeros((acc.shape[0], IDX_DIM), F32)

    def hi_lo(x):
        hi = x.astype(BF16).astype(F32)
        return hi, (x - hi).astype(BF16).astype(F32)

    pieces = []
    for g in range(IDX_HEADS * IDX_DIM // LANES):
        blk = _rope_lanes(acc[:, g * LANES:(g + 1) * LANES], c, s1, s2, half)
        for u in range(LANES // IDX_DIM):
            hi, lo = hi_lo(blk[:, u * IDX_DIM:(u + 1) * IDX_DIM])
            pieces += [hi, lo, hi, zeros]
    qcat_ref[...] = jnp.concatenate(pieces, axis=1).astype(BF16)
    last = acc[:, IDX_HEADS * IDX_DIM:]
    ki = _rope_lanes(last, c, s1, s2, half)[:, :IDX_DIM]
    kf_ref[...] = ki
    hi, lo = hi_lo(ki)
    kcat_ref[...] = jnp.concatenate([hi, hi, lo, zeros], axis=1).astype(BF16)
    wi_ref[...] = last[:, IDX_DIM:IDX_DIM + IDX_HEADS] * (IDX_HEADS ** -0.5 * IDX_DIM ** -0.5)


def _idx_proj(hh, hl, wh, wl, tabs):
    m, d = hh.shape
    tm = _row_tile(m)
    ntab = tabs[0].shape[0] // tm
    row = lambda w: pl.BlockSpec((tm, w), lambda i: (i, 0))
    full = lambda a: pl.BlockSpec(a.shape, lambda i: (0, 0))
    tab_spec = pl.BlockSpec((tm, LANES), lambda i: (i % ntab, 0))
    return pl.pallas_call(
        _idx_proj_kernel,
        out_shape=(jax.ShapeDtypeStruct((m, IDX_HEADS * 256), BF16), jax.ShapeDtypeStruct((m, 256), BF16),
                   jax.ShapeDtypeStruct((m, IDX_DIM), F32), jax.ShapeDtypeStruct((m, IDX_HEADS), F32)),
        grid=(m // tm,),
        in_specs=[row(d), row(d), full(wh), full(wl), tab_spec, tab_spec, tab_spec],
        out_specs=(row(IDX_HEADS * 256), row(256), row(IDX_DIM), row(IDX_HEADS)),
        compiler_params=_cparams(("parallel",)),
        name="idx_proj",
    )(hh, hl, wh, wl, *tabs)


def _sort_key(s):
    bits = lax.bitcast_convert_type(s, I32)
    return bits ^ (lax.shift_right_arithmetic(bits, 31) & 0x7FFFFFFF)


def _kth_largest_16(vals, need):
    r, l = vals.shape

    def count_ge(cand):
        one = jnp.where(vals >= cand, jnp.int16(1), jnp.int16(0))
        acc = one[:, 0:LANES]
        for b in range(1, l // LANES):
            acc = acc + one[:, b * LANES:(b + 1) * LANES]
        return jnp.sum(acc.astype(F32), axis=1, keepdims=True)

    def body(it, prefix):
        cand_u = prefix | lax.shift_left(jnp.int32(1), 15 - it)
        cand = (cand_u - 32768).astype(I16)
        return jnp.where(count_ge(cand) >= need, cand_u, prefix)

    return lax.fori_loop(0, 16, body, jnp.zeros((r, 1), I32), unroll=True) - 32768


def _kth_largest_key(key, n_sel):
    hi = lax.shift_right_arithmetic(key, 16)
    lo = (key & 0xFFFF) - 32768
    t_hi = _kth_largest_16(hi.astype(I16), jnp.float32(n_sel))
    n_above = jnp.sum(jnp.where(hi > t_hi, 1.0, 0.0), axis=1, keepdims=True)
    lo_in = jnp.where(hi == t_hi, lo, -32768).astype(I16)
    t_lo = _kth_largest_16(lo_in, n_sel - n_above)
    return lax.shift_left(t_hi, 16) | (t_lo + 32768)


def _tie_select(key, thr, n_sel, blk):
    r, l = key.shape
    gt = key > thr
    eq = key == thr
    need = n_sel - jnp.sum(jnp.where(gt, 1.0, 0.0), axis=1, keepdims=True)
    rr = lax.broadcasted_iota(I32, (blk, blk), 0)
    cc = lax.broadcasted_iota(I32, (blk, blk), 1)
    upper = jnp.where(rr < cc, 1.0, 0.0).astype(BF16)
    offs = jnp.zeros((r, 1), F32)
    parts = []
    for b in range(l // blk):
        e = jnp.where(eq[:, b * blk:(b + 1) * blk], 1.0, 0.0)
        rank = _dot(e.astype(BF16), upper) + offs
        parts.append(jnp.logical_and(eq[:, b * blk:(b + 1) * blk], rank < need))
        offs = offs + jnp.sum(e, axis=1, keepdims=True)
    return jnp.logical_or(gt, jnp.concatenate(parts, axis=1))


def _selection_bias(s, causal, n_sel, blk, bias_ref):
    key = _sort_key(s)
    thr = _kth_largest_key(key, n_sel)
    ge_c = jnp.logical_and(key >= thr, causal)
    cnt = (jnp.sum(jnp.where(ge_c, 1.0, 0.0), axis=1, keepdims=True)
           + jnp.sum(jnp.where(jnp.logical_or(key <= thr, causal), 0.0, 1.0), axis=1, keepdims=True))
    bias_ref[...] = jnp.where(ge_c, 0.0, NEG)

    @pl.when(jnp.max(cnt) > n_sel)
    def _():
        sel = _tie_select(key, thr, n_sel, blk)
        bias_ref[...] = jnp.where(jnp.logical_and(sel, causal), 0.0, NEG)


def _attn_prompt_kernel(q_ref, k_ref, v_ref, qi_ref, ki_ref, wi_ref, o_ref, bias_ref, *, n_sel, q0):
    tq = q_ref.shape[1]
    width = k_ref.shape[1]
    ki = ki_ref[0]
    wi = wi_ref[0]
    score = jnp.zeros((tq, width), F32)
    for h in range(IDX_HEADS):
        lg = _dot_nt(qi_ref[0, :, h * 256:(h + 1) * 256], ki)
        score = score + wi[:, h:h + 1] * jnp.maximum(lg, 0.0)
    col = lax.broadcasted_iota(I32, (tq, width), 1)
    row = q0 + lax.broadcasted_iota(I32, (tq, width), 0)
    causal = col <= row
    _selection_bias(jnp.where(causal, score, NEG), causal, n_sel, RANK_BLK, bias_ref)
    bias = bias_ref[...]
    for h in range(N_HEADS_A):
        sl = slice(h * HEAD_A, (h + 1) * HEAD_A)
        s = _dot_nt(q_ref[0, :, sl], k_ref[0, :, sl]) + bias
        m = jnp.max(s, axis=1, keepdims=True)
        p = jnp.exp(s - m)
        l = jnp.sum(p, axis=1, keepdims=True)
        o = _dot(p.astype(BF16), v_ref[0, :, sl]) / l
        o_ref[0, :, sl] = o.astype(o_ref.dtype)


def _attn_prompt(q, k, v, qcat, kcat, wi):
    b, s_len, _ = q.shape
    tq = min(TQ, s_len)
    n_sel = min(TOPK, s_len // 4)
    outs = []
    for i in range(s_len // tq):
        width = (i + 1) * tq
        blk_q = lambda w, i=i: pl.BlockSpec((1, tq, w), lambda bi: (bi, i, 0))
        blk_k = lambda w, width=width: pl.BlockSpec((1, width, w), lambda bi: (bi, 0, 0))
        outs.append(pl.pallas_call(
            functools.partial(_attn_prompt_kernel, n_sel=n_sel, q0=i * tq),
            out_shape=jax.ShapeDtypeStruct((b, tq, C_A), BF16),
            grid=(b,),
            in_specs=[blk_q(C_A), blk_k(C_A), blk_k(C_A), blk_q(IDX_HEADS * 256), blk_k(256), blk_q(IDX_HEADS)],
            out_specs=pl.BlockSpec((1, tq, C_A), lambda bi: (bi, 0, 0)),
            scratch_shapes=[pltpu.VMEM((tq, width), F32)],
            compiler_params=_cparams(("parallel",)),
            name="attn_prompt_%d" % i,
        )(q, k, v, qcat, kcat, wi))
    return jnp.concatenate(outs, axis=1)


SEL_PAGES = 16
ATT_PAGES = 16


def _score_sample_kernel(pt_ref, qh_ref, ql_ref, w_ref, knew_ref, *rest, past):
    pages = rest[:SEL_PAGES]
    sc_ref = rest[SEL_PAGES]
    step = pl.program_id(1)
    qh, ql, w = qh_ref[0], ql_ref[0], w_ref[0]
    nq = qh.shape[0] // IDX_HEADS

    def scores_of(kt):
        kh, kl = _split_bf16(kt)
        lg = _dot(qh, kh) + _dot(ql, kh) + _dot(qh, kl)
        x = w * jnp.maximum(lg, 0.0)
        return jnp.sum(x.reshape(nq, IDX_HEADS, kt.shape[1]), axis=1)

    @pl.when(step == 0)
    def _():
        sc_ref[0] = jnp.zeros(sc_ref.shape[1:], F32)
        sc_ref[0, 0:nq, past:past + PAGE] = scores_of(knew_ref[0])

    span = SEL_PAGES * PAGE
    off = pl.multiple_of(step * span, span)
    sc_ref[0, 0:nq, pl.ds(off, span)] = scores_of(jnp.concatenate([pg[0, 0] for pg in pages], axis=1))


def _select_sample_kernel(s_ref, mask_ref, bias_ref, *, n_sel, past, t_new, nq):
    s = s_ref[...]
    col = lax.broadcasted_iota(I32, s.shape, 1)
    qpos = past + (lax.broadcasted_iota(I32, s.shape, 0) & (nq - 1))
    causal = col <= qpos
    s = jnp.where(causal, s, NEG)
    s = jnp.where(col < past + t_new, s, -jnp.inf)
    _selection_bias(s, causal, n_sel, LANES, bias_ref)
    mask_ref[...] = jnp.where(bias_ref[...] > -1.0, 1.0, 0.0)


def _sel_sample(page_table, layer, qh, ql, w, knew, cache_ki, t_new):
    b, n_pages = page_table.shape
    past = n_pages * PAGE
    n_sel = min(TOPK, (past + t_new) // 4)
    rows = qh.shape[1]
    nq = rows // IDX_HEADS
    assert nq & (nq - 1) == 0
    width = past + PAGE
    steps = n_pages // SEL_PAGES
    cache_ki = jnp.swapaxes(cache_ki, 2, 3)
    knew = jnp.swapaxes(knew, 1, 2)
    page_spec = lambda u: pl.BlockSpec((1, 1, IDX_DIM, PAGE),
                                       lambda bi, s, pt: (layer, pt[bi, s * SEL_PAGES + u], 0, 0))
    per_b = lambda shp: pl.BlockSpec((1,) + shp, lambda bi, s, pt: (bi, 0, 0))
    scores = pl.pallas_call(
        functools.partial(_score_sample_kernel, past=past),
        out_shape=jax.ShapeDtypeStruct((b, 8, width), F32),
        grid_spec=pltpu.PrefetchScalarGridSpec(
            num_scalar_prefetch=1,
            grid=(b, steps),
            in_specs=[per_b((rows, IDX_DIM)), per_b((rows, IDX_DIM)), per_b((rows, 1)), per_b((IDX_DIM, PAGE))]
                     + [page_spec(u) for u in range(SEL_PAGES)],
            out_specs=per_b((8, width)),
        ),
        compiler_params=_cparams(("parallel", "arbitrary")),
        name="score_sample",
    )(page_table, qh, ql, w, knew, *([cache_ki] * SEL_PAGES))
    flat = scores[:, :nq].reshape(b * nq, width)
    mask = pl.pallas_call(
        functools.partial(_select_sample_kernel, n_sel=n_sel, past=past, t_new=t_new, nq=nq),
        out_shape=jax.ShapeDtypeStruct((b * nq, width), F32),
        scratch_shapes=[pltpu.VMEM((b * nq, width), F32)],
        compiler_params=pltpu.CompilerParams(vmem_limit_bytes=VMEM_LIMIT),
        name="select_sample",
    )(flat)
    return jnp.pad(mask.reshape(b, nq, width), ((0, 0), (0, MASK_ROWS - nq), (0, 0)))


def _attn_sample_kernel(pt_ref, q_ref, knew_ref, vnew_ref, mnew_ref, mask_ref, *rest, nq):
    kpages = rest[:ATT_PAGES]
    vpages = rest[ATT_PAGES:2 * ATT_PAGES]
    o_ref = rest[2 * ATT_PAGES]
    m_ref, l_ref, acc_ref = rest[2 * ATT_PAGES + 1:]
    step = pl.program_id(1)
    q = q_ref[0]
    rows = q.shape[0]
    span = PAGE * N_HEADS_A
    tok = lax.broadcasted_iota(I32, (PAGE, span), 0)
    ln = lax.broadcasted_iota(I32, (PAGE, span), 1)
    expand = jnp.where(lax.shift_right_logical(ln, 3) == tok, 1.0, 0.0).astype(BF16)

    def valid_of(mk, width):
        e = _dot(mk.astype(BF16), expand[:, :width])
        sel = jnp.broadcast_to(e[:nq][:, None, :], (nq, N_HEADS_A, width)).reshape(rows, width) > 0.5
        lane_h = lax.broadcasted_iota(I32, (rows, width), 1) & (N_HEADS_A - 1)
        row_h = lax.broadcasted_iota(I32, (rows, width), 0) & (N_HEADS_A - 1)
        return jnp.logical_and(sel, lane_h == row_h)

    def update(kb, vb, valid, first):
        s = jnp.where(valid, _dot_nt(q, kb), NEG)
        m_blk = jnp.max(s, axis=1, keepdims=True)
        if first:
            m_new = m_blk
            p = jnp.where(valid, jnp.exp(s - m_new), 0.0)
            l_ref[...] = jnp.sum(p, axis=1, keepdims=True)
            acc_ref[...] = _dot(p.astype(BF16), vb)
        else:
            m_old = m_ref[...]
            m_new = jnp.maximum(m_old, m_blk)
            alpha = jnp.exp(m_old - m_new)
            p = jnp.where(valid, jnp.exp(s - m_new), 0.0)
            l_ref[...] = alpha * l_ref[...] + jnp.sum(p, axis=1, keepdims=True)
            acc_ref[...] = alpha * acc_ref[...] + _dot(p.astype(BF16), vb)
        m_ref[...] = m_new

    @pl.when(step == 0)
    def _():
        update(knew_ref[0], vnew_ref[0], valid_of(mnew_ref[0], PAGE), True)

    @pl.when(step > 0)
    def _():
        kb = jnp.concatenate([pg[0, 0].reshape(span, HEAD_A).astype(BF16) for pg in kpages], axis=0)
        vb = jnp.concatenate([pg[0, 0].reshape(span, HEAD_A).astype(BF16) for pg in vpages], axis=0)
        valid = jnp.concatenate([valid_of(mask_ref[0, :, u * PAGE:(u + 1) * PAGE], span)
                                 for u in range(ATT_PAGES)], axis=1)
        update(kb, vb, valid, False)

    @pl.when(step == pl.num_programs(1) - 1)
    def _():
        o_ref[0] = acc_ref[...] / l_ref[...]


def _attn_sample(page_table, layer, q, knew, vnew, mask, cache_k, cache_v):
    b, n_pages = page_table.shape
    rows = q.shape[1]
    nq = rows // N_HEADS_A
    past = n_pages * PAGE
    steps = n_pages // ATT_PAGES + 1
    pg = lambda s, u: jnp.maximum(s - 1, 0) * ATT_PAGES + u
    page_spec = lambda u: pl.BlockSpec((1, 1, PAGE, N_HEADS_A, HEAD_A),
                                       lambda bi, s, pt: (layer, pt[bi, pg(s, u)], 0, 0, 0))
    per_b = lambda shp: pl.BlockSpec((1,) + shp, lambda bi, s, pt: (bi, 0, 0))
    return pl.pallas_call(
        functools.partial(_attn_sample_kernel, nq=nq),
        out_shape=jax.ShapeDtypeStruct((b, rows, HEAD_A), F32),
        grid_spec=pltpu.PrefetchScalarGridSpec(
            num_scalar_prefetch=1,
            grid=(b, steps),
            in_specs=[per_b((rows, HEAD_A)), per_b((PAGE, HEAD_A)), per_b((PAGE, HEAD_A)),
                      pl.BlockSpec((1, MASK_ROWS, PAGE), lambda bi, s, pt: (bi, 0, past // PAGE)),
                      pl.BlockSpec((1, MASK_ROWS, ATT_PAGES * PAGE),
                                   lambda bi, s, pt: (bi, 0, jnp.maximum(s - 1, 0)))]
                     + [page_spec(u) for u in range(ATT_PAGES)] * 2,
            out_specs=per_b((rows, HEAD_A)),
            scratch_shapes=[pltpu.VMEM((rows, 1), F32), pltpu.VMEM((rows, 1), F32),
                            pltpu.VMEM((rows, HEAD_A), F32)],
        ),
        compiler_params=_cparams(("parallel", "arbitrary")),
        name="attn_sample",
    )(page_table, q, knew, vnew, mask, mask, *([cache_k] * ATT_PAGES), *([cache_v] * ATT_PAGES))


def _seg_sum(x, jmat):
    r = x.shape[0]
    xs = jnp.concatenate([x[:, p * LANES:(p + 1) * LANES] for p in range(N_PAIRS)], axis=0)
    hi, lo = _split_bf16(xs)
    ys = _dot(hi, jmat) + _dot(lo, jmat)
    return jnp.concatenate([ys[p * r:(p + 1) * r] for p in range(N_PAIRS)], axis=1)


def _rwkv_kernel(pb_ref, first_ref, s0_ref, mu_ref, w0_ref, a0_ref, kk_ref, ka_ref, rk_ref, lw_ref, lb_ref,
                 w2h, w2l, a2h, a2l, g2h, g2l, ob_ref, st_ref, carry_ref, state_ref, *, t_valid):
    nb, c = pb_ref.shape[0], pb_ref.shape[1]
    rows = nb * c
    n = pl.program_id(1)
    half = HEAD_B

    @pl.when(n == 0)
    def _():
        carry_ref[...] = first_ref[:, 0, :]
        z = jnp.zeros((half, half), F32)
        for b in range(nb):
            for p in range(N_PAIRS):
                top = jnp.concatenate([s0_ref[b, 2 * p], z], axis=1)
                bot = jnp.concatenate([z, s0_ref[b, 2 * p + 1]], axis=1)
                state_ref[b * N_PAIRS + p] = jnp.concatenate([top, bot], axis=0)

    pb = pb_ref[...].reshape(rows, PB_W)
    rowi = lax.broadcasted_iota(I32, (rows, 1), 0)
    step_i = rowi & (c - 1)
    carry = jnp.broadcast_to(carry_ref[...][:, None, :], (nb, c, PB_W)).reshape(rows, PB_W)
    prev = jnp.where(step_i == 0, carry, pltpu.roll(pb, 1, axis=0))
    carry_ref[...] = pb_ref[:, c - 1, :]
    xm = pb + (prev - pb) * mu_ref[...]

    r = xm[:, 0:C_B]
    k = xm[:, C_B:2 * C_B]
    v = xm[:, 2 * C_B:3 * C_B]
    xw = xm[:, 3 * C_B:3 * C_B + 128]
    xa = xm[:, 3 * C_B + 128:3 * C_B + 256]
    xg = xm[:, 3 * C_B + 256:3 * C_B + 512]
    w = -jax.nn.softplus(-(w0_ref[...] + _dot3w(jnp.tanh(xw), w2h, w2l))) - 0.5
    ld = -jnp.exp(w)
    a = jax.nn.sigmoid(a0_ref[...] + _dot3w(xa, a2h, a2l))
    g = _dot3w(jax.nn.sigmoid(xg), g2h, g2l)

    li = lax.broadcasted_iota(I32, (LANES, LANES), 0)
    lj = lax.broadcasted_iota(I32, (LANES, LANES), 1)
    same_head = (li // half) == (lj // half)
    jmat = jnp.where(same_head, 1.0, 0.0).astype(BF16)

    kkr = k * kk_ref[...]
    kk = kkr * lax.rsqrt(jnp.maximum(_seg_sum(kkr * kkr, jmat), 1e-24))
    k2 = k * (1.0 + (a - 1.0) * ka_ref[...])
    bonus = _seg_sum(r * k2 * rk_ref[...], jmat)
    avec = -kk
    bvec = kk * a
    if t_valid < c:
        live = step_i < t_valid
        ld = jnp.where(live, ld, 0.0)
        avec = jnp.where(live, avec, 0.0)
        bvec = jnp.where(live, bvec, 0.0)
        k2s = jnp.where(live, k2, 0.0)
        vs = jnp.where(live, v, 0.0)
    else:
        k2s, vs = k2, v

    ti = lax.broadcasted_iota(I32, (rows, rows), 0)
    tj = lax.broadcasted_iota(I32, (rows, rows), 1)
    tri = jnp.where(jnp.logical_and(tj <= ti, (ti // c) == (tj // c)), 1.0, 0.0).astype(BF16)
    l1 = ld.astype(BF16)
    rem = ld - l1.astype(F32)
    l2 = rem.astype(BF16)
    l3 = (rem - l2.astype(F32)).astype(BF16)
    logc = _dot(tri, l1) + _dot(tri, l2) + _dot(tri, l3)
    logc_end = jnp.broadcast_to(logc.reshape(nb, c, C_B)[:, c - 1:c, :], (nb, c, C_B)).reshape(rows, C_B)
    e_pos = jnp.exp(logc)
    e_neg = jnp.exp(-logc)
    e_end = jnp.exp(logc_end - logc)

    def grp(x):
        return jnp.stack([x[b * c:(b + 1) * c, p * LANES:(p + 1) * LANES]
                          for b in range(nb) for p in range(N_PAIRS)], axis=0)

    rt = grp(r * e_pos)
    at = grp(avec * jnp.exp(logc - ld))
    kt = grp(k2s * e_neg)
    bt = grp(bvec * e_neg)
    kc = grp(k2s * e_end)
    bc = grp(bvec * e_end)
    vg = grp(vs)
    dec_end = grp(jnp.exp(logc_end))[:, 0:1, :]

    lane = lax.broadcasted_iota(I32, (1, 1, LANES), 2)
    m0 = lane < half
    col4 = lax.broadcasted_iota(I32, (1, c, 4 * c), 2) & (c - 1)
    row4 = lax.broadcasted_iota(I32, (1, c, 4 * c), 1)
    eye2 = jnp.where(lax.broadcasted_iota(I32, (1, 2 * c, 2 * c), 1)
                     == lax.broadcasted_iota(I32, (1, 2 * c, 2 * c), 2), 1.0, 0.0)
    lane2c = lax.broadcasted_iota(I32, (1, 1, 2 * c), 2) < c

    def two(x):
        return jnp.concatenate([jnp.where(m0, x, 0.0), jnp.where(m0, 0.0, x)], axis=1)

    s_old = state_ref[...]
    lhs1 = jnp.concatenate([rt, at], axis=1).astype(BF16)
    rhs1 = jnp.concatenate([two(kt), two(bt)], axis=1).astype(BF16)
    amat = _bmm_nt(lhs1, rhs1)
    a_r = jnp.where(col4 <= row4, amat[:, :c], 0.0)
    a_a = jnp.where(col4 < row4, amat[:, c:], 0.0)
    x0 = _bmm_nt(lhs1, s_old.astype(BF16))
    vm = two(vg).astype(BF16)
    x = x0[:, c:] + _bmm(a_a[:, :, :2 * c].astype(BF16), vm)
    lab = a_a[:, :, 2 * c:]
    lbd = jnp.concatenate([jnp.where(lane2c, lab, 0.0), jnp.where(lane2c, 0.0, lab)], axis=1)
    pm = eye2 + lbd
    lk = lbd
    for _ in range(int(np.log2(c)) - 1):
        lkb = lk.astype(BF16)
        lk = _bmm(lkb, lkb)
        pm = pm + _bmm(pm.astype(BF16), lk.astype(BF16))
    us = _bmm(pm.astype(BF16), two(x).astype(BF16))
    u = us[:, :c] + us[:, c:]
    y = x0[:, :c] + _bmm(a_r.astype(BF16), jnp.concatenate([vm, two(u).astype(BF16)], axis=1))
    vals = jnp.concatenate([vg, u], axis=1)
    keys = jnp.concatenate([kc, bc], axis=1).astype(BF16)
    add = _bmm(jnp.swapaxes(vals, 1, 2).astype(BF16), keys)
    state_ref[...] = s_old * dec_end + jnp.where(same_head[None], add, 0.0)

    y = jnp.concatenate([jnp.concatenate([y[b * N_PAIRS + p] for p in range(N_PAIRS)], axis=1)
                         for b in range(nb)], axis=0)
    mean = _seg_sum(y, jmat) * (1.0 / half)
    d = y - mean
    var = _seg_sum(d * d, jmat) * (1.0 / half)
    yn = d * lax.rsqrt(var + LNX_EPS) * lw_ref[...] + lb_ref[...]
    ob_ref[...] = ((yn + bonus * v) * g).astype(ob_ref.dtype).reshape(nb, c, C_B)

    @pl.when(n == pl.num_programs(1) - 1)
    def _():
        for b in range(nb):
            for p in range(N_PAIRS):
                s_new = state_ref[b * N_PAIRS + p]
                st_ref[b, 2 * p] = s_new[:half, :half]
                st_ref[b, 2 * p + 1] = s_new[half:, half:]


def _rwkv(pb, first, s0, vecs, loras, t_valid):
    b, t, _ = pb.shape
    nb = int(np.gcd(b, RWKV_NB))
    vec = lambda a: pl.BlockSpec(a.shape, lambda bi, n: (0, 0))
    st_spec = pl.BlockSpec((nb,) + s0.shape[1:], lambda bi, n: (bi, 0, 0, 0))
    return pl.pallas_call(
        functools.partial(_rwkv_kernel, t_valid=t_valid),
        out_shape=(jax.ShapeDtypeStruct((b, t, C_B), BF16), jax.ShapeDtypeStruct(s0.shape, F32)),
        grid=(b // nb, t // CHUNK),
        in_specs=[pl.BlockSpec((nb, CHUNK, PB_W), lambda bi, n: (bi, n, 0)),
                  pl.BlockSpec((nb, 1, PB_W), lambda bi, n: (bi, 0, 0)), st_spec]
                 + [vec(a) for a in vecs] + [vec(a) for a in loras],
        out_specs=(pl.BlockSpec((nb, CHUNK, C_B), lambda bi, n: (bi, n, 0)), st_spec),
        scratch_shapes=[pltpu.VMEM((nb, PB_W), F32), pltpu.VMEM((nb * N_PAIRS, LANES, LANES), F32)],
        compiler_params=_cparams(("parallel", "arbitrary")),
        name="rwkv7",
    )(pb, first, s0, *vecs, *loras)


def _merge_kernel(oa_ref, ob_ref, wpa_ref, wpb_ref, ga_ref, gb_ref, o_ref):
    za = _dot(oa_ref[...], wpa_ref[...])
    zb = _dot(ob_ref[...], wpb_ref[...])
    o_ref[...] = (ga_ref[...].astype(F32) * za + gb_ref[...].astype(F32) * zb).astype(o_ref.dtype)


def _merge(oa, ob, wpa, wpb, gates, tn=1024):
    m = oa.shape[0]
    tm = _row_tile(m, 1024)
    nb = D_MODEL // tn
    return pl.pallas_call(
        _merge_kernel,
        out_shape=jax.ShapeDtypeStruct((m, D_MODEL), BF16),
        grid=(m // tm, nb),
        in_specs=[pl.BlockSpec((tm, C_A), lambda i, j: (i, 0)), pl.BlockSpec((tm, C_B), lambda i, j: (i, 0)),
                  pl.BlockSpec((C_A, tn), lambda i, j: (0, j)), pl.BlockSpec((C_B, tn), lambda i, j: (0, j)),
                  pl.BlockSpec((tm, tn), lambda i, j: (i, j)), pl.BlockSpec((tm, tn), lambda i, j: (i, j + nb))],
        out_specs=pl.BlockSpec((tm, tn), lambda i, j: (i, j)),
        compiler_params=_cparams(("parallel", "parallel")),
        name="merge",
    )(oa, ob, wpa, wpb, gates, gates)


def _oproj_kernel(x_ref, mix_ref, wo_ref, g_ref, x1_ref, h2_ref):
    x1 = x_ref[...] + _dot(mix_ref[...], wo_ref[...])
    x1_ref[...] = x1
    h2 = x1 * lax.rsqrt(jnp.mean(x1 * x1, axis=-1, keepdims=True) + NORM_EPS) * g_ref[...]
    h2_ref[...] = h2.astype(BF16)


def _oproj(x, mixed, wo, g):
    m, d = x.shape
    tm = _row_tile(m)
    row = pl.BlockSpec((tm, d), lambda i: (i, 0))
    return pl.pallas_call(
        _oproj_kernel,
        out_shape=(jax.ShapeDtypeStruct((m, d), F32), jax.ShapeDtypeStruct((m, d), BF16)),
        grid=(m // tm,),
        in_specs=[row, row, pl.BlockSpec((d, d), lambda i: (0, 0)), pl.BlockSpec((1, d), lambda i: (0, 0))],
        out_specs=(row, row),
        compiler_params=_cparams(("parallel",)),
        name="oproj_norm",
    )(x, mixed, wo, g.reshape(1, d))


def _ffn_up_kernel(h_ref, wg_ref, wu_ref, o_ref, wg_bf, wu_bf):
    @pl.when(pl.program_id(1) == 0)
    def _():
        wg_bf[...] = wg_ref[...].astype(BF16)
        wu_bf[...] = wu_ref[...].astype(BF16)

    h = h_ref[...]
    gate = _dot(h, wg_bf[...])
    up = _dot(h, wu_bf[...])
    o_ref[...] = (gate * jax.nn.sigmoid(gate) * up).astype(o_ref.dtype)


def _ffn_up(h2, wg, wu, tn=512):
    m, d = h2.shape
    n = wg.shape[1]
    tm = _row_tile(m, 1024)
    return pl.pallas_call(
        _ffn_up_kernel,
        out_shape=jax.ShapeDtypeStruct((m, n), BF16),
        grid=(n // tn, m // tm),
        in_specs=[pl.BlockSpec((tm, d), lambda j, i: (i, 0)), pl.BlockSpec((d, tn), lambda j, i: (0, j)),
                  pl.BlockSpec((d, tn), lambda j, i: (0, j))],
        out_specs=pl.BlockSpec((tm, tn), lambda j, i: (i, j)),
        scratch_shapes=[pltpu.VMEM((d, tn), BF16), pltpu.VMEM((d, tn), BF16)],
        compiler_params=_cparams(("arbitrary", "arbitrary")),
        name="ffn_up",
    )(h2, wg, wu)


def _ffn_down_kernel(x_ref, a_ref, wd_ref, o_ref, wd_bf):
    @pl.when(pl.program_id(1) == 0)
    def _():
        wd_bf[...] = wd_ref[...].astype(BF16)

    o_ref[...] = x_ref[...] + _dot(a_ref[...], wd_bf[...])


def _ffn_down(x1, act, wd, tn=512):
    m, d = x1.shape
    kdim = act.shape[1]
    tm = _row_tile(m)
    return pl.pallas_call(
        _ffn_down_kernel,
        out_shape=jax.ShapeDtypeStruct((m, d), F32),
        grid=(d // tn, m // tm),
        in_specs=[pl.BlockSpec((tm, tn), lambda j, i: (i, j)), pl.BlockSpec((tm, kdim), lambda j, i: (i, 0)),
                  pl.BlockSpec((kdim, tn), lambda j, i: (0, j))],
        out_specs=pl.BlockSpec((tm, tn), lambda j, i: (i, j)),
        scratch_shapes=[pltpu.VMEM((kdim, tn), BF16)],
        compiler_params=_cparams(("arbitrary", "arbitrary")),
        name="ffn_down",
    )(x1, act, wd)


def _pad_cols(w, width):
    return jnp.pad(w, ((0, 0), (0, width - w.shape[1])))


def _pad_rows(w, height):
    return jnp.pad(w, ((0, height - w.shape[0]), (0, 0)))


def _prep_weights(w_in, mu_b, w2, a2, g2):
    o = 0
    cols = {}
    for name, wdt in (("q", C_A), ("k", C_A), ("v", C_A), ("qi", IDX_HEADS * IDX_DIM), ("ki", IDX_DIM),
                      ("wi", IDX_HEADS), ("g", 2 * D_MODEL), ("rkv", 3 * C_B), ("xw", D_DECAY), ("xa", D_AAA),
                      ("xg", D_GATE)):
        cols[name] = (o, o + wdt)
        o += wdt
    cut = lambda name: w_in[:, cols[name][0]:cols[name][1]]
    w_idx = _pad_cols(jnp.concatenate([cut("qi"), cut("ki"), cut("wi")], axis=1), IDX_W)
    idx_hi, idx_lo = _split_bf16(w_idx)
    w_b = jnp.concatenate([cut("rkv"), _pad_cols(cut("xw"), 128), _pad_cols(cut("xa"), 128),
                           _pad_cols(cut("xg"), 256)], axis=1)
    nb0 = cols["rkv"][0]
    mu = lambda name: mu_b[cols[name][0] - nb0:cols[name][1] - nb0][None, :]
    mu_p = jnp.concatenate([mu("rkv"), _pad_cols(mu("xw"), 128), _pad_cols(mu("xa"), 128),
                            _pad_cols(mu("xg"), 256)], axis=1)
    loras = []
    for wl, hgt in ((w2, 128), (a2, 128), (g2, 256)):
        loras += list(_split_bf16(_pad_rows(wl, hgt)))
    return dict(q=cut("q").astype(BF16), k=cut("k").astype(BF16), v=cut("v").astype(BF16),
                idx_hi=idx_hi, idx_lo=idx_lo, g=cut("g").astype(BF16), b=w_b.astype(BF16), mu=mu_p,
                loras=loras)


def _layer(x, pos, wts, p, *, seq, attn_fn, first_pb, s0, t_valid):
    m = x.shape[0]
    nb = m // seq
    hh, hl = _norm_split(x, p["g_mix"])
    tabs_a = _rope_tables(pos, HEAD_A)
    tabs_i = _rope_tables(pos, IDX_DIM)
    (q_b,) = _proj(hh, wts["q"], tabs_a, rope=True, want_f32=False, want_bf16=True, scale=HEAD_A ** -0.5)
    k_f, k_b = _proj(hh, wts["k"], tabs_a, rope=True, want_f32=True, want_bf16=True)
    v_f, v_b = _proj(hh, wts["v"], tabs_a, rope=False, want_f32=True, want_bf16=True)
    qcat, kcat, ki_f, wi = _idx_proj(hh, hl, wts["idx_hi"], wts["idx_lo"], tabs_i)
    gates = _mm(hh, wts["g"], tn=1024, out_dtype=BF16, act="sigmoid")
    pb = _mm(hh, wts["b"], tn=PB_W // 4, out_dtype=F32)

    o_a = attn_fn(q_b, k_b, v_b, qcat, kcat, ki_f, wi)

    t_pad = -(-seq // CHUNK) * CHUNK
    pb3 = pb.reshape(nb, seq, PB_W)
    if t_pad != seq:
        pb3 = jnp.pad(pb3, ((0, 0), (0, t_pad - seq), (0, 0)))
    vecs = [wts["mu"]] + [p[n].reshape(1, C_B) for n in ("w0", "a0", "k_k", "k_a", "r_k", "lnx_w", "lnx_b")]
    o_b, wkv = _rwkv(pb3, first_pb, s0, vecs, wts["loras"], t_valid)
    o_b = o_b[:, :seq].reshape(m, C_B)

    mixed = _merge(o_a, o_b, p["w_pa"], p["w_pb"], gates)
    x1, h2 = _oproj(x, mixed, p["w_o"], p["g_ffn"])
    act = _ffn_up(h2, p["w_gate"], p["w_up"])
    x2 = _ffn_down(x1, act, p["w_down"])
    shift = _norm_f32(x.reshape(nb, seq, D_MODEL)[:, -1], p["g_mix"])
    return x2, k_f, v_f, ki_f, wkv, shift


def kernel(x_prompt, x_sample, cache_k, cache_v, cache_kidx, state_wkv, state_shift, page_table, g_mix, w_in, mu_b, w0, w2, a0, a2, g2, k_k, k_a, r_k, lnx_w, lnx_b, w_pa, w_pb, w_o, g_ffn, w_gate, w_up, w_down, g_final):
    bp, sp, d = x_prompt.shape
    bs, ts, _ = x_sample.shape
    depth = w_in.shape[0]
    n_pages = page_table.shape[1]
    past = n_pages * PAGE
    xp = x_prompt.reshape(bp * sp, d)
    xs = x_sample.reshape(bs * ts, d)
    outs_p, outs_s = [], []
    for l in range(depth):
        wts = _prep_weights(w_in[l], mu_b[l], w2[l], a2[l], g2[l])
        p = dict(g_mix=g_mix[l], w0=w0[l], a0=a0[l], k_k=k_k[l], k_a=k_a[l], r_k=r_k[l], lnx_w=lnx_w[l],
                 lnx_b=lnx_b[l], w_pa=w_pa[l].astype(BF16), w_pb=w_pb[l].astype(BF16), w_o=w_o[l].astype(BF16),
                 g_ffn=g_ffn[l], w_gate=w_gate[l], w_up=w_up[l], w_down=w_down[l])

        def attn_p(q_b, k_b, v_b, qcat, kcat, ki_f, wi):
            r3 = lambda a: a.reshape(bp, sp, a.shape[-1])
            return _attn_prompt(r3(q_b), r3(k_b), r3(v_b), r3(qcat), r3(kcat), r3(wi)).reshape(bp * sp, C_A)

        def attn_s(q_b, k_b, v_b, qcat, kcat, ki_f, wi):
            qc = qcat.reshape(bs, ts * IDX_HEADS, 256)
            qh, ql = qc[..., :IDX_DIM], qc[..., IDX_DIM:2 * IDX_DIM]
            wrow = wi.reshape(bs, ts * IDX_HEADS, 1)
            ki_new = jnp.pad(ki_f.reshape(bs, ts, IDX_DIM), ((0, 0), (0, PAGE - ts), (0, 0)))
            mask = _sel_sample(page_table, l, qh, ql, wrow, ki_new, cache_kidx, ts)
            as_rows = lambda a: a.reshape(bs, ts * N_HEADS_A, HEAD_A)
            pad_rows = lambda a: jnp.pad(as_rows(a), ((0, 0), (0, PAGE - ts * N_HEADS_A), (0, 0)))
            o = _attn_sample(page_table, l, as_rows(q_b), pad_rows(k_b), pad_rows(v_b), mask, cache_k, cache_v)
            return o.reshape(bs * ts, C_A).astype(BF16)

        zero_first = jnp.zeros((bp, 1, PB_W), F32)
        zero_state = jnp.zeros((bp, N_HEADS_B, HEAD_B, HEAD_B), F32)
        xp, kp, vp, kip, wkvp, shp = _layer(xp, jnp.arange(sp), wts, p, seq=sp, attn_fn=attn_p,
                                            first_pb=zero_first, s0=zero_state, t_valid=CHUNK)
        first_s = _mm(state_shift[l].astype(BF16), wts["b"], tn=PB_W // 4, out_dtype=F32).reshape(bs, 1, PB_W)
        pos_s = jnp.tile(past + jnp.arange(ts), bs)
        xs, ksm, vsm, kis, wkvs, shs = _layer(xs, pos_s, wts, p, seq=ts, attn_fn=attn_s,
                                              first_pb=first_s, s0=state_wkv[l], t_valid=ts)
        outs_p.append((kp.reshape(bp, sp, N_HEADS_A, HEAD_A), vp.reshape(bp, sp, N_HEADS_A, HEAD_A),
                       kip.reshape(bp, sp, IDX_DIM), wkvp, shp))
        outs_s.append((ksm.reshape(bs, ts, N_HEADS_A, HEAD_A), vsm.reshape(bs, ts, N_HEADS_A, HEAD_A),
                       kis.reshape(bs, ts, IDX_DIM), wkvs, shs))
    y_prompt = _norm_f32(xp, g_final).reshape(bp, sp, d)
    y_sample = _norm_f32(xs, g_final).reshape(bs, ts, d)
    stack = lambda outs, i: jnp.stack([o[i] for o in outs])
    return ((y_prompt, y_sample) + tuple(stack(outs_p, i) for i in range(5))
            + tuple(stack(outs_s, i) for i in range(5)))
```
